```python
import jax, jax.numpy as jnp
from jax import lax
import numpy as np

D_MODEL = 1024
BATCH = 1
SEQ = 16384
DEPTH = 1
DEC_BATCH = 16
DEC_SEQ = 2048
PAST_LEN = 128

HEAD_DIM = 64
N_HEADS_A = 8
N_HEADS_B = 8
N_KV_B = 2
WIDTH_A = N_HEADS_A * HEAD_DIM
WIDTH_B = N_HEADS_B * HEAD_DIM
KV_WIDTH_B = N_KV_B * HEAD_DIM
MIX_WIDTH = WIDTH_A + WIDTH_B
IN_WIDTH = 3 * WIDTH_A + WIDTH_B + 2 * KV_WIDTH_B
DILATED_CONFIGS = ((128, 1), (512, 4), (2048, 16))
WINDOW_B = 128
D_FF = 2816
CONV_WIDTH = 3
EPS = 1e-6
NEG_INF = -1e30

kernel_name = 'hymba_dilated_swa_convffn_encoder'


def alibi_slopes():
    n = N_HEADS_A + N_HEADS_B
    h = np.arange(1, n + 1, dtype=np.float32)
    s = np.exp2(-8.0 * h / n).astype(np.float32)
    return jnp.asarray(s[N_HEADS_B:]), jnp.asarray(s[:N_HEADS_B])


def rms_norm(x, g):
    xf = x.astype(jnp.float32)
    y = xf * lax.rsqrt(jnp.mean(xf * xf, axis=-1, keepdims=True) + EPS)
    return (y * g.astype(jnp.float32)).astype(x.dtype)


def banded_attention(q, k, v, key_valid, radius, dist_scale, slopes, sink=None):
    n, L, G, R, dh = q.shape
    blk = radius
    nb = -(-L // blk)
    pad = nb * blk - L
    q = jnp.pad(q, ((0, 0), (0, pad), (0, 0), (0, 0), (0, 0)))
    kv_pad = ((0, 0), (blk, pad + blk), (0, 0), (0, 0))
    k = jnp.pad(k, kv_pad)
    v = jnp.pad(v, kv_pad)
    valid = jnp.pad(key_valid, ((0, 0), (blk, pad + blk)))

    def windows(a):
        a = a.reshape((n, nb + 2, blk) + a.shape[2:])
        return jnp.concatenate([a[:, :-2], a[:, 1:-1], a[:, 2:]], axis=2)

    kw, vw, mw = windows(k), windows(v), windows(valid)
    qb = q.reshape(n, nb, blk, G, R, dh)
    s = jnp.einsum('nbqgrd,nbkgd->nbgrqk', qb, kw).astype(jnp.float32) * (dh ** -0.5)
    rel = jnp.arange(3 * blk)[None, :] - blk - jnp.arange(blk)[:, None]
    dist = jnp.abs(rel).astype(jnp.float32) * dist_scale
    s = s - slopes.astype(jnp.float32)[:, :, None, None] * dist
    allowed = (jnp.abs(rel) <= radius)[None, None, None, None] & mw[:, :, None, None, None, :]
    s = jnp.where(allowed, s, NEG_INF)
    m = jnp.max(s, axis=-1)
    if sink is not None:
        sk = sink.astype(jnp.float32)[None, None, :, :, None]
        m = jnp.maximum(m, sk)
    p = jnp.exp(s - m[..., None])
    denom = jnp.sum(p, axis=-1)
    if sink is not None:
        denom = denom + jnp.exp(sk - m)
    o = jnp.einsum('nbgrqk,nbkgd->nbqgrd', p.astype(v.dtype), vw).astype(jnp.float32)
    denom_t = jnp.transpose(denom, (0, 1, 4, 2, 3))
    m_t = jnp.transpose(m, (0, 1, 4, 2, 3))
    o = o / denom_t[..., None]
    lse = m_t + jnp.log(denom_t)
    o = o.reshape(n, nb * blk, G, R, dh)[:, :L]
    lse = lse.reshape(n, nb * blk, G, R)[:, :L]
    return o, lse


def dilated_attention(q, k, v, slopes):
    b, S, H, dh = q.shape
    outs, lses = [], []
    for window, d in DILATED_CONFIGS:
        radius = window // (2 * d)
        Sd = -(-S // d) * d
        pad = ((0, 0), (0, Sd - S), (0, 0), (0, 0))

        def split(a):
            a = a.reshape((b, Sd // d, d) + a.shape[2:])
            a = jnp.swapaxes(a, 1, 2)
            return a.reshape((b * d, Sd // d) + a.shape[3:])

        def merge(a):
            a = a.reshape((b, d, Sd // d) + a.shape[2:])
            a = jnp.swapaxes(a, 1, 2)
            return a.reshape((b, Sd) + a.shape[3:])[:, :S]

        valid = jnp.broadcast_to(jnp.arange(Sd) < S, (b, Sd))
        qs = split(jnp.pad(q, pad))[:, :, :, None]
        ks = split(jnp.pad(k, pad))
        vs = split(jnp.pad(v, pad))
        o, lse = banded_attention(qs, ks, vs, split(valid), radius, float(d), slopes[:, None])
        outs.append(merge(o[:, :, :, 0]))
        lses.append(merge(lse[:, :, :, 0]))
    w = jax.nn.softmax(jnp.stack(lses, axis=0), axis=0)
    return jnp.sum(w[..., None] * jnp.stack(outs, axis=0), axis=0)


def encoder_layer(x, norm1, w_in, q_norm_a, k_norm_a, q_norm_b, k_norm_b, sink_b,
                  out_norm_a, out_norm_b, w_out, norm2, w_up, conv_w, conv_b, w_down):
    b, S, _ = x.shape
    slopes_a, slopes_b = alibi_slopes()
    h = rms_norm(x, norm1)
    proj = h @ w_in
    cuts = [WIDTH_A, 2 * WIDTH_A, 3 * WIDTH_A, 3 * WIDTH_A + WIDTH_B, 3 * WIDTH_A + WIDTH_B + KV_WIDTH_B]
    qa, ka, va, qb, kb, vb = jnp.split(proj, cuts, axis=-1)

    qa = rms_norm(qa.reshape(b, S, N_HEADS_A, HEAD_DIM), q_norm_a)
    ka = rms_norm(ka.reshape(b, S, N_HEADS_A, HEAD_DIM), k_norm_a)
    va = va.reshape(b, S, N_HEADS_A, HEAD_DIM)
    ya = dilated_attention(qa, ka, va, slopes_a).astype(x.dtype).reshape(b, S, WIDTH_A)

    rep = N_HEADS_B // N_KV_B
    qb = rms_norm(qb.reshape(b, S, N_KV_B, rep, HEAD_DIM), q_norm_b)
    kb = rms_norm(kb.reshape(b, S, N_KV_B, HEAD_DIM), k_norm_b)
    vb = vb.reshape(b, S, N_KV_B, HEAD_DIM)
    valid = jnp.ones((b, S), dtype=bool)
    yb, _ = banded_attention(qb, kb, vb, valid, WINDOW_B, 1.0,
                             slopes_b.reshape(N_KV_B, rep), sink_b.reshape(N_KV_B, rep))
    yb = yb.astype(x.dtype).reshape(b, S, WIDTH_B)

    y = jnp.concatenate([rms_norm(ya, out_norm_a), rms_norm(yb, out_norm_b)], axis=-1)
    x = x + y @ w_out

    u = rms_norm(x, norm2) @ w_up
    half = CONV_WIDTH // 2
    up = jnp.pad(u, ((0, 0), (half, half), (0, 0)))
    c = conv_b
    for j in range(CONV_WIDTH):
        c = c + up[:, j:j + S] * conv_w[j]
    gate, val = jnp.split(c, 2, axis=-1)
    return x + (jax.nn.silu(gate) * val) @ w_down


def setup_inputs(seed: int = 0) -> dict:
    key = jax.random.key(seed)
    ks = jax.random.split(key, 20)
    f32 = jnp.float32

    def nrm(k, shape, scale):
        return jax.random.normal(k, shape, f32) * scale

    def gain(k, shape):
        return 1.0 + 0.02 * jax.random.normal(k, shape, f32)

    return {
        'x_prompt': nrm(ks[0], (BATCH, SEQ, D_MODEL), 1.0),
        'x_sample': nrm(ks[1], (DEC_BATCH, DEC_SEQ, D_MODEL), 1.0),
        'norm1': gain(ks[2], (DEPTH, D_MODEL)),
        'w_in': nrm(ks[3], (DEPTH, D_MODEL, IN_WIDTH), D_MODEL ** -0.5),
        'q_norm_a': gain(ks[4], (DEPTH, HEAD_DIM)),
        'k_norm_a': gain(ks[5], (DEPTH, HEAD_DIM)),
        'q_norm_b': gain(ks[6], (DEPTH, HEAD_DIM)),
        'k_norm_b': gain(ks[7], (DEPTH, HEAD_DIM)),
        'sink_b': nrm(ks[8], (DEPTH, N_HEADS_B), 0.1),
        'out_norm_a': gain(ks[9], (DEPTH, WIDTH_A)),
        'out_norm_b': gain(ks[10], (DEPTH, WIDTH_B)),
        'w_out': nrm(ks[11], (DEPTH, MIX_WIDTH, D_MODEL), MIX_WIDTH ** -0.5),
        'norm2': gain(ks[12], (DEPTH, D_MODEL)),
        'w_up': nrm(ks[13], (DEPTH, D_MODEL, 2 * D_FF), D_MODEL ** -0.5),
        'conv_w': nrm(ks[14], (DEPTH, CONV_WIDTH, 2 * D_FF), CONV_WIDTH ** -0.5),
        'conv_b': nrm(ks[15], (DEPTH, 2 * D_FF), 0.02),
        'w_down': nrm(ks[16], (DEPTH, D_FF, D_MODEL), D_FF ** -0.5),
    }


def reference(x_prompt, x_sample, norm1, w_in, q_norm_a, k_norm_a, q_norm_b, k_norm_b, sink_b,
              out_norm_a, out_norm_b, w_out, norm2, w_up, conv_w, conv_b, w_down):
    y_prompt = x_prompt
    y_sample = x_sample
    for l in range(DEPTH):
        layer_args = (norm1[l], w_in[l], q_norm_a[l], k_norm_a[l], q_norm_b[l], k_norm_b[l], sink_b[l],
                      out_norm_a[l], out_norm_b[l], w_out[l], norm2[l], w_up[l], conv_w[l], conv_b[l], w_down[l])
        y_prompt = encoder_layer(y_prompt, *layer_args)
        y_sample = encoder_layer(y_sample, *layer_args)
    return (y_prompt, y_sample)
```

```python
import functools

import numpy as np
import jax
import jax.numpy as jnp
from jax import lax
from jax.experimental import pallas as pl
from jax.experimental.pallas import tpu as pltpu

F32 = jnp.float32
BF16 = jnp.bfloat16

D_MODEL = 1024
HEAD_DIM = 64
PAIR = 2 * HEAD_DIM
N_PAIRS = 4
WIDTH_A = 512
WIDTH_B = 512
KV_WIDTH_B = 128
IN_WIDTH = 2304
DILATIONS = (1, 4, 16)
RADIUS_A = 64
RADIUS_B = 128
D_FF = 2816
EPS = 1e-6
MASKED = 1e30

Q_BLOCK = 128
TOKEN_TILE_IN = 512
TOKEN_TILE_ATTN = 2048
TOKEN_TILE_FFN = 512
FFN_HALO = 16
FF_CHUNK = 256
N_FF_CHUNKS = D_FF // FF_CHUNK
VMEM_LIMIT = 56 * 1024 * 1024


def _alibi_slopes():
    n = 16
    h = np.arange(1, n + 1, dtype=np.float32)
    s = np.exp2(-8.0 * h / n).astype(np.float32)
    return s[8:], s[:8]


def _inproj_kernel(x_ref, n1_ref, w_ref, seg_ref, rep_ref, gqa_ref, gka_ref, gqb_ref, gkb_ref,
                   qa_ref, ka_ref, va_ref, qb_ref, kb_ref, vb_ref):
    x = x_ref[...]
    ms = jnp.mean(x * x, axis=-1, keepdims=True)
    h = ((x * lax.rsqrt(ms + EPS)) * n1_ref[...]).astype(BF16)
    seg = seg_ref[...]

    def proj(c0, n):
        return jnp.dot(h, w_ref[:, c0:c0 + n], preferred_element_type=F32)

    def head_norm(p, g, seg_m):
        msq = jnp.dot((p * p).astype(BF16), seg_m, preferred_element_type=F32)
        return (p * lax.rsqrt(msq + EPS)) * g

    def write_pairs(ref, y, first_pair):
        for j in range(y.shape[1] // PAIR):
            ref[first_pair + j] = y[:, j * PAIR:(j + 1) * PAIR].astype(BF16)

    for c in range(2):
        write_pairs(qa_ref, head_norm(proj(c * 256, 256), gqa_ref[...], seg), 2 * c)
        write_pairs(ka_ref, head_norm(proj(512 + c * 256, 256), gka_ref[...], seg), 2 * c)
        write_pairs(va_ref, proj(1024 + c * 256, 256), 2 * c)
        write_pairs(qb_ref, head_norm(proj(1536 + c * 256, 256), gqb_ref[...], seg), 2 * c)

    rep = rep_ref[...]
    kb = head_norm(proj(2048, 128), gkb_ref[...], seg[:PAIR, :PAIR]).astype(BF16)
    vb = proj(2176, 128).astype(BF16)
    write_pairs(kb_ref, jnp.dot(kb, rep, preferred_element_type=F32), 0)
    write_pairs(vb_ref, jnp.dot(vb, rep, preferred_element_type=F32), 0)


def _inproj(x2, n1, w_in, seg, rep, gqa, gka, gqb, gkb):
    n = x2.shape[0]
    t = TOKEN_TILE_IN
    const = lambda i: (0, 0)
    pair_out = lambda p: pl.BlockSpec((p, t, PAIR), lambda i: (0, i, 0))
    return pl.pallas_call(
        _inproj_kernel,
        grid=(n // t,),
        in_specs=[
            pl.BlockSpec((t, D_MODEL), lambda i: (i, 0)),
            pl.BlockSpec((1, D_MODEL), const),
            pl.BlockSpec((D_MODEL, IN_WIDTH), const),
            pl.BlockSpec((256, 256), const),
            pl.BlockSpec((PAIR, 256), const),
            pl.BlockSpec((1, 256), const),
            pl.BlockSpec((1, 256), const),
            pl.BlockSpec((1, 256), const),
            pl.BlockSpec((1, PAIR), const),
        ],
        out_specs=[pair_out(4), pair_out(4), pair_out(4), pair_out(4), pair_out(2), pair_out(2)],
        out_shape=[jax.ShapeDtypeStruct((p, n, PAIR), BF16) for p in (4, 4, 4, 4, 2, 2)],
        compiler_params=pltpu.CompilerParams(
            dimension_semantics=("arbitrary",), vmem_limit_bytes=VMEM_LIMIT),
        name="inproj",
    )(x2, n1, w_in, seg, rep, gqa, gka, gqb, gkb)


def _build_bias(bias_ref, cfg, slope0, slope1, radius, dist_scale):
    width = Q_BLOCK + 2 * radius
    row = lax.broadcasted_iota(jnp.int32, (Q_BLOCK, width), 0)
    col = lax.broadcasted_iota(jnp.int32, (Q_BLOCK, width), 1)
    rel = col - radius - row
    band = jnp.abs(rel) <= radius
    dist = jnp.abs(rel).astype(F32) * dist_scale
    for variant in range(4):
        ok = band
        if variant & 1:
            ok = ok & (col >= radius)
        if variant & 2:
            ok = ok & (col < Q_BLOCK + radius)
        base = jnp.where(ok, dist, MASKED)
        bias_ref[cfg, variant, 0] = base * slope0
        bias_ref[cfg, variant, 1] = base * slope1


def _pair_scores(qblk, kwin, vwin, bias_ref, cfg, variant, sinks=None):
    lane = lax.broadcasted_iota(jnp.int32, (1, PAIR), 1)
    out = None
    m_pair = None
    for h in range(2):
        mine = (lane < HEAD_DIM) if h == 0 else (lane >= HEAD_DIM)
        keep = mine.astype(BF16)
        kh = kwin * keep
        s = lax.dot_general(qblk, kh, (((1,), (1,)), ((), ())), preferred_element_type=F32)
        s = s - bias_ref[cfg, variant, h]
        m = jnp.max(s, axis=-1, keepdims=True)
        if sinks is not None:
            m = jnp.maximum(m, sinks[h])
        p = jnp.exp(s - m).astype(BF16)
        rhs = jnp.concatenate([vwin * keep, jnp.broadcast_to(keep, vwin.shape)], axis=1)
        o = jnp.dot(p, rhs, preferred_element_type=F32)
        out = o if out is None else out + o
        m_b = jnp.broadcast_to(m, (Q_BLOCK, PAIR))
        m_pair = m_b if m_pair is None else jnp.where(mine, m_b, m_pair)
    return out[:, :PAIR], out[:, PAIR:], m_pair


def _fill_window(win_ref, prev_ref, cur_ref, next_ref, radius):
    rows = cur_ref.shape[0]
    win_ref[0:radius, :] = prev_ref[...]
    win_ref[radius:radius + rows, :] = cur_ref[...]
    win_ref[radius + rows:radius + rows + radius, :] = next_ref[...]


def _mixer_a_kernel(slopes_ref, *refs, n_tiles):
    nd = len(DILATIONS)
    q_refs = refs[0:nd]
    k_refs = refs[nd:4 * nd]
    v_refs = refs[4 * nd:7 * nd]
    o_ref = refs[7 * nd]
    scratch = refs[7 * nd + 1:]
    kw_refs = scratch[0:nd]
    vw_refs = scratch[nd:2 * nd]
    bias_ref, acc_ref, den_ref, max_ref = scratch[2 * nd:]

    hp = pl.program_id(0)
    b = pl.program_id(1)
    i = pl.program_id(2)
    t = TOKEN_TILE_ATTN

    @pl.when((b == 0) & (i == 0))
    def _():
        for c, d in enumerate(DILATIONS):
            _build_bias(bias_ref, c, slopes_ref[2 * hp], slopes_ref[2 * hp + 1], RADIUS_A, float(d))

    for c, d in enumerate(DILATIONS):
        _fill_window(kw_refs[c], *k_refs[3 * c:3 * c + 3], RADIUS_A)
        _fill_window(vw_refs[c], *v_refs[3 * c:3 * c + 3], RADIUS_A)
        n_blocks = t // d // Q_BLOCK
        for r in range(d):
            cols = slice(r * PAIR, (r + 1) * PAIR)

            def block(qb, carry, c=c, d=d, cols=cols, r=r, n_blocks=n_blocks):
                q0 = qb * Q_BLOCK if n_blocks == 1 else pl.multiple_of(qb * Q_BLOCK, Q_BLOCK)
                first = (i == 0) & (qb == 0)
                last = (i == n_tiles - 1) & (qb == n_blocks - 1)
                variant = first.astype(jnp.int32) + 2 * last.astype(jnp.int32)
                qblk = q_refs[c][pl.ds(q0, Q_BLOCK), cols]
                kwin = kw_refs[c][pl.ds(q0, Q_BLOCK + 2 * RADIUS_A), cols]
                vwin = vw_refs[c][pl.ds(q0, Q_BLOCK + 2 * RADIUS_A), cols]
                acc, den, mx = _pair_scores(qblk, kwin, vwin, bias_ref, c, variant)
                if d == 1:
                    rows = pl.ds(q0, Q_BLOCK)
                else:
                    rows = pl.ds(r + d * q0, Q_BLOCK, stride=d)
                acc_ref[c, rows, :] = acc
                den_ref[c, rows, :] = den
                max_ref[c, rows, :] = mx
                return carry

            if n_blocks == 1:
                block(0, 0)
            else:
                lax.fori_loop(0, n_blocks, block, 0)

    chunk = 256

    def merge(j, carry):
        rows = pl.ds(pl.multiple_of(j * chunk, chunk), chunk)
        ms = [max_ref[c, rows, :] for c in range(nd)]
        top = functools.reduce(jnp.maximum, ms)
        num = None
        den = None
        for c in range(nd):
            w = jnp.exp(ms[c] - top)
            n_c = w * acc_ref[c, rows, :]
            d_c = w * den_ref[c, rows, :]
            num = n_c if num is None else num + n_c
            den = d_c if den is None else den + d_c
        o_ref[rows, :] = num / den
        return carry

    lax.fori_loop(0, t // chunk, merge, 0)


def _halo_specs(d, rows, total_rows, radius, lead_map):
    per = rows // radius
    last = total_rows // radius - 1
    width = d * PAIR

    def prev(*g):
        return (*lead_map(*g), jnp.maximum(g[-1] * per - 1, 0), 0)

    def cur(*g):
        return (*lead_map(*g), g[-1], 0)

    def nxt(*g):
        return (*lead_map(*g), jnp.minimum((g[-1] + 1) * per, last), 0)

    return [pl.BlockSpec((None, None, radius, width), prev),
            pl.BlockSpec((None, None, rows, width), cur),
            pl.BlockSpec((None, None, radius, width), nxt)]


def _mixer_a(qa, ka, va, slopes, batch, seq):
    t = TOKEN_TILE_ATTN
    n_tiles = seq // t
    lead = lambda hp, b, i: (hp, b)
    views = lambda a: [a.reshape(N_PAIRS, batch, seq // d, d * PAIR) for d in DILATIONS]
    in_specs = [pl.BlockSpec(memory_space=pltpu.SMEM)]
    for d in DILATIONS:
        in_specs.append(pl.BlockSpec((None, None, t // d, d * PAIR), lambda hp, b, i: (hp, b, i, 0)))
    halo = []
    for d in DILATIONS:
        halo += _halo_specs(d, t // d, seq // d, RADIUS_A, lead)
    in_specs += halo + halo
    k_views = [v for kv in views(ka) for v in (kv, kv, kv)]
    v_views = [v for vv in views(va) for v in (vv, vv, vv)]
    scratch = [pltpu.VMEM((t // d + 2 * RADIUS_A, d * PAIR), BF16) for d in DILATIONS] * 2
    scratch += [pltpu.VMEM((len(DILATIONS), 4, 2, Q_BLOCK, Q_BLOCK + 2 * RADIUS_A), F32)]
    scratch += [pltpu.VMEM((len(DILATIONS), t, PAIR), F32)] * 3
    return pl.pallas_call(
        functools.partial(_mixer_a_kernel, n_tiles=n_tiles),
        grid=(N_PAIRS, batch, n_tiles),
        in_specs=in_specs,
        out_specs=pl.BlockSpec((None, t, PAIR), lambda hp, b, i: (b, i, hp)),
        out_shape=jax.ShapeDtypeStruct((batch, seq, WIDTH_A), F32),
        scratch_shapes=scratch,
        compiler_params=pltpu.CompilerParams(
            dimension_semantics=("arbitrary", "arbitrary", "arbitrary"), vmem_limit_bytes=VMEM_LIMIT),
        name="mixer_a",
    )(slopes, *views(qa), *k_views, *v_views)


def _mixer_b_kernel(slopes_ref, sink_ref, q_ref, kp_ref, kc_ref, kn_ref, vp_ref, vc_ref, vn_ref,
                    o_ref, kw_ref, vw_ref, bias_ref, *, n_tiles):
    hp = pl.program_id(0)
    b = pl.program_id(1)
    i = pl.program_id(2)
    t = TOKEN_TILE_ATTN
    n_blocks = t // Q_BLOCK

    @pl.when((b == 0) & (i == 0))
    def _():
        _build_bias(bias_ref, 0, slopes_ref[2 * hp], slopes_ref[2 * hp + 1], RADIUS_B, 1.0)

    _fill_window(kw_ref, kp_ref, kc_ref, kn_ref, RADIUS_B)
    _fill_window(vw_ref, vp_ref, vc_ref, vn_ref, RADIUS_B)
    sinks = (sink_ref[2 * hp], sink_ref[2 * hp + 1])
    lane = lax.broadcasted_iota(jnp.int32, (1, PAIR), 1)
    sink_pair = jnp.where(lane < HEAD_DIM, sinks[0], sinks[1])

    def block(qb, carry):
        q0 = pl.multiple_of(qb * Q_BLOCK, Q_BLOCK)
        first = (i == 0) & (qb == 0)
        last = (i == n_tiles - 1) & (qb == n_blocks - 1)
        variant = first.astype(jnp.int32) + 2 * last.astype(jnp.int32)
        qblk = q_ref[pl.ds(q0, Q_BLOCK), :]
        kwin = kw_ref[pl.ds(q0, Q_BLOCK + 2 * RADIUS_B), :]
        vwin = vw_ref[pl.ds(q0, Q_BLOCK + 2 * RADIUS_B), :]
        acc, den, mx = _pair_scores(qblk, kwin, vwin, bias_ref, 0, variant, sinks=sinks)
        den = den + jnp.exp(sink_pair - mx)
        o_ref[pl.ds(q0, Q_BLOCK), :] = acc / den
        return carry

    lax.fori_loop(0, n_blocks, block, 0)


def _mixer_b(qb, kb, vb, slopes, sinks, batch, seq):
    t = TOKEN_TILE_ATTN
    n_tiles = seq // t
    kv_lead = lambda hp, b, i: (hp // 2, b)
    view = lambda a: a.reshape(a.shape[0], batch, seq, PAIR)
    halo = _halo_specs(1, t, seq, RADIUS_B, kv_lead)
    smem = pl.BlockSpec(memory_space=pltpu.SMEM)
    return pl.pallas_call(
        functools.partial(_mixer_b_kernel, n_tiles=n_tiles),
        grid=(N_PAIRS, batch, n_tiles),
        in_specs=[smem, smem, pl.BlockSpec((None, None, t, PAIR), lambda hp, b, i: (hp, b, i, 0))]
        + halo + halo,
        out_specs=pl.BlockSpec((None, t, PAIR), lambda hp, b, i: (b, i, hp)),
        out_shape=jax.ShapeDtypeStruct((batch, seq, WIDTH_B), F32),
        scratch_shapes=[pltpu.VMEM((t + 2 * RADIUS_B, PAIR), BF16)] * 2
        + [pltpu.VMEM((1, 4, 2, Q_BLOCK, Q_BLOCK + 2 * RADIUS_B), F32)],
        compiler_params=pltpu.CompilerParams(
            dimension_semantics=("arbitrary", "arbitrary", "arbitrary"), vmem_limit_bytes=VMEM_LIMIT),
        name="mixer_b",
    )(slopes, sinks, view(qb), *([view(kb)] * 3), *([view(vb)] * 3))


def _rms(x, g):
    ms = jnp.mean(x * x, axis=-1, keepdims=True)
    return (x * lax.rsqrt(ms + EPS)) * g


def _outffn_kernel(yap_ref, yac_ref, yan_ref, ybp_ref, ybc_ref, ybn_ref, xp_ref, xc_ref, xn_ref,
                   ga_ref, gb_ref, wo_ref, n2_ref, wug_ref, wuv_ref, cwg_ref, cwv_ref, cbg_ref, cbv_ref,
                   wd_ref, o_ref, h2_ref, *, tiles_per_seq):
    i = pl.program_id(0)
    t = TOKEN_TILE_FFN
    halo = FFN_HALO
    first = (i % tiles_per_seq) == 0
    last = (i % tiles_per_seq) == tiles_per_seq - 1

    def mixed_rows(ya_ref, yb_ref, x_ref):
        na = _rms(ya_ref[...], ga_ref[...]).astype(BF16)
        nb = _rms(yb_ref[...], gb_ref[...]).astype(BF16)
        return (x_ref[...]
                + jnp.dot(na, wo_ref[0:WIDTH_A, :], preferred_element_type=F32)
                + jnp.dot(nb, wo_ref[WIDTH_A:, :], preferred_element_type=F32))

    h_prev = _rms(mixed_rows(yap_ref, ybp_ref, xp_ref), n2_ref[...])
    h2_ref[0:halo, :] = jnp.where(first, 0.0, h_prev).astype(BF16)
    h_next = _rms(mixed_rows(yan_ref, ybn_ref, xn_ref), n2_ref[...])
    h2_ref[halo + t:, :] = jnp.where(last, 0.0, h_next).astype(BF16)
    x1 = mixed_rows(yac_ref, ybc_ref, xc_ref)
    h2_ref[halo:halo + t, :] = _rms(x1, n2_ref[...]).astype(BF16)
    o_ref[...] = x1

    rows = t + 2 * halo

    def conv(u, w_ref, b_ref, j):
        w = w_ref[j]
        below = pltpu.roll(u, 1, axis=0)
        above = pltpu.roll(u, rows - 1, axis=0)
        c = b_ref[j] + below[halo:halo + t] * w[0:1]
        c = c + u[halo:halo + t] * w[1:2]
        return c + above[halo:halo + t] * w[2:3]

    def chunk(j, carry):
        h2 = h2_ref[...]
        gate = conv(jnp.dot(h2, wug_ref[j], preferred_element_type=F32), cwg_ref, cbg_ref, j)
        val = conv(jnp.dot(h2, wuv_ref[j], preferred_element_type=F32), cwv_ref, cbv_ref, j)
        hidden = (jax.nn.silu(gate) * val).astype(BF16)
        o_ref[...] += jnp.dot(hidden, wd_ref[j], preferred_element_type=F32)
        return carry

    lax.fori_loop(0, N_FF_CHUNKS, chunk, 0)


def _outffn(ya, yb, x2, ga, gb, w_out, n2, wug, wuv, cwg, cwv, cbg, cbv, wd, seq):
    n = x2.shape[0]
    t = TOKEN_TILE_FFN
    per = t // FFN_HALO
    last = n // FFN_HALO - 1
    prev = lambda i: (jnp.maximum(i * per - 1, 0), 0)
    cur = lambda i: (i, 0)
    nxt = lambda i: (jnp.minimum((i + 1) * per, last), 0)

    def halo_specs(width):
        return [pl.BlockSpec((FFN_HALO, width), prev), pl.BlockSpec((t, width), cur),
                pl.BlockSpec((FFN_HALO, width), nxt)]

    def resident(shape):
        return pl.BlockSpec(shape, lambda i: (0,) * len(shape), pipeline_mode=pl.Buffered(1))

    return pl.pallas_call(
        functools.partial(_outffn_kernel, tiles_per_seq=seq // t),
        grid=(n // t,),
        in_specs=halo_specs(WIDTH_A) + halo_specs(WIDTH_B) + halo_specs(D_MODEL) + [
            resident((1, WIDTH_A)), resident((1, WIDTH_B)), resident((D_MODEL, D_MODEL)),
            resident((1, D_MODEL)),
            resident((N_FF_CHUNKS, D_MODEL, FF_CHUNK)), resident((N_FF_CHUNKS, D_MODEL, FF_CHUNK)),
            resident((N_FF_CHUNKS, 3, FF_CHUNK)), resident((N_FF_CHUNKS, 3, FF_CHUNK)),
            resident((N_FF_CHUNKS, 1, FF_CHUNK)), resident((N_FF_CHUNKS, 1, FF_CHUNK)),
            resident((N_FF_CHUNKS, FF_CHUNK, D_MODEL)),
        ],
        out_specs=pl.BlockSpec((t, D_MODEL), cur),
        out_shape=jax.ShapeDtypeStruct((n, D_MODEL), F32),
        scratch_shapes=[pltpu.VMEM((t + 2 * FFN_HALO, D_MODEL), BF16)],
        compiler_params=pltpu.CompilerParams(
            dimension_semantics=("arbitrary",), vmem_limit_bytes=VMEM_LIMIT),
        name="outffn",
    )(ya, ya, ya, yb, yb, yb, x2, x2, x2, ga, gb, w_out, n2, wug, wuv, cwg, cwv, cbg, cbv, wd)


def _prepare(norm1, w_in, q_norm_a, k_norm_a, q_norm_b, k_norm_b, sink_b, out_norm_a, out_norm_b, w_out,
             norm2, w_up, conv_w, conv_b, w_down):
    scale = HEAD_DIM ** -0.5
    lanes = np.arange(256)
    seg = (lanes[:, None] // HEAD_DIM == lanes[None, :] // HEAD_DIM).astype(np.float32) / HEAD_DIM
    src = np.arange(PAIR)
    rep = ((src[:, None] // HEAD_DIM == lanes[None, :] // PAIR)
           & (src[:, None] % HEAD_DIM == lanes[None, :] % HEAD_DIM)).astype(np.float32)
    slopes_a, slopes_b = _alibi_slopes()

    def chunks(a):
        a = a.reshape(a.shape[:-1] + (N_FF_CHUNKS, FF_CHUNK))
        return jnp.moveaxis(a, -2, 0)

    return dict(
        n1=norm1.reshape(1, D_MODEL),
        w_in=w_in.astype(BF16),
        seg=jnp.asarray(seg, BF16),
        rep=jnp.asarray(rep, BF16),
        gqa=(jnp.tile(q_norm_a, 4) * scale).reshape(1, 256),
        gka=jnp.tile(k_norm_a, 4).reshape(1, 256),
        gqb=(jnp.tile(q_norm_b, 4) * scale).reshape(1, 256),
        gkb=jnp.tile(k_norm_b, 2).reshape(1, PAIR),
        slopes_a=jnp.asarray(slopes_a),
        slopes_b=jnp.asarray(slopes_b),
        sinks=sink_b.astype(F32),
        ga=out_norm_a.reshape(1, WIDTH_A),
        gb=out_norm_b.reshape(1, WIDTH_B),
        w_out=w_out.astype(BF16),
        n2=norm2.reshape(1, D_MODEL),
        wug=chunks(w_up[:, :D_FF]).astype(BF16),
        wuv=chunks(w_up[:, D_FF:]).astype(BF16),
        cwg=chunks(conv_w[:, :D_FF]),
        cwv=chunks(conv_w[:, D_FF:]),
        cbg=chunks(conv_b[:D_FF].reshape(1, D_FF)),
        cbv=chunks(conv_b[D_FF:].reshape(1, D_FF)),
        wd=w_down.reshape(N_FF_CHUNKS, FF_CHUNK, D_MODEL).astype(BF16),
    )


def _layer(x, p):
    batch, seq, _ = x.shape
    assert seq % TOKEN_TILE_ATTN == 0 and seq % TOKEN_TILE_FFN == 0
    x2 = x.reshape(batch * seq, D_MODEL)
    qa, ka, va, qb, kb, vb = _inproj(x2, p["n1"], p["w_in"], p["seg"], p["rep"],
                                     p["gqa"], p["gka"], p["gqb"], p["gkb"])
    ya = _mixer_a(qa, ka, va, p["slopes_a"], batch, seq).reshape(batch * seq, WIDTH_A)
    yb = _mixer_b(qb, kb, vb, p["slopes_b"], p["sinks"], batch, seq).reshape(batch * seq, WIDTH_B)
    out = _outffn(ya, yb, x2, p["ga"], p["gb"], p["w_out"], p["n2"], p["wug"], p["wuv"],
                  p["cwg"], p["cwv"], p["cbg"], p["cbv"], p["wd"], seq)
    return out.reshape(batch, seq, D_MODEL)


def kernel(x_prompt, x_sample, norm1, w_in, q_norm_a, k_norm_a, q_norm_b, k_norm_b, sink_b, out_norm_a,
           out_norm_b, w_out, norm2, w_up, conv_w, conv_b, w_down):
    y_prompt, y_sample = x_prompt, x_sample
    for l in range(norm1.shape[0]):
        p = _prepare(norm1[l], w_in[l], q_norm_a[l], k_norm_a[l], q_norm_b[l], k_norm_b[l], sink_b[l],
                     out_norm_a[l], out_norm_b[l], w_out[l], norm2[l], w_up[l], conv_w[l], conv_b[l],
                     w_down[l])
        y_prompt = _layer(y_prompt, p)
        y_sample = _layer(y_sample, p)
    return (y_prompt, y_sample)
```

```python
import functools

import numpy as np
import jax
import jax.numpy as jnp
from jax import lax
from jax.experimental import pallas as pl
from jax.experimental.pallas import tpu as pltpu

F32 = jnp.float32
BF16 = jnp.bfloat16

D_MODEL = 1024
HEAD_DIM = 64
PAIR = 2 * HEAD_DIM
N_PAIRS = 4
WIDTH_A = 512
WIDTH_B = 512
KV_WIDTH_B = 128
IN_WIDTH = 2304
DILATIONS = (1, 4, 16)
MERGE_STRIDE = 4
RADIUS_A = 64
RADIUS_B = 128
D_FF = 2816
EPS = 1e-6
MASKED = 1e30

Q_BLOCK = 128
BLOCKS_IN_FLIGHT = 8
TOKEN_TILE_IN = 512
TOKEN_TILE_ATTN = 2048
TOKEN_TILE_FFN = 512
FFN_HALO = 16
FF_CHUNK = 256
N_FF_CHUNKS = D_FF // FF_CHUNK
VMEM_LIMIT = 56 * 1024 * 1024


def _alibi_slopes():
    n = 16
    h = np.arange(1, n + 1, dtype=np.float32)
    s = np.exp2(-8.0 * h / n).astype(np.float32)
    return s[8:], s[:8]


def _inproj_kernel(x_ref, n1_ref, w_ref, seg_ref, rep_ref, gqa_ref, gka_ref, gqb_ref, gkb_ref,
                   qa_ref, ka_ref, va_ref, qb_ref, kb_ref, vb_ref):
    x = x_ref[...]
    ms = jnp.mean(x * x, axis=-1, keepdims=True)
    h = ((x * lax.rsqrt(ms + EPS)) * n1_ref[...]).astype(BF16)
    seg = seg_ref[...]

    def proj(c0, n):
        return jnp.dot(h, w_ref[:, c0:c0 + n], preferred_element_type=F32)

    def head_norm(p, g, seg_m):
        msq = jnp.dot((p * p).astype(BF16), seg_m, preferred_element_type=F32)
        return (p * lax.rsqrt(msq + EPS)) * g

    def write_pairs(ref, y, first_pair):
        for j in range(y.shape[1] // PAIR):
            ref[first_pair + j] = y[:, j * PAIR:(j + 1) * PAIR].astype(BF16)

    for c in range(2):
        write_pairs(qa_ref, head_norm(proj(c * 256, 256), gqa_ref[...], seg), 2 * c)
        write_pairs(ka_ref, head_norm(proj(512 + c * 256, 256), gka_ref[...], seg), 2 * c)
        write_pairs(va_ref, proj(1024 + c * 256, 256), 2 * c)
        write_pairs(qb_ref, head_norm(proj(1536 + c * 256, 256), gqb_ref[...], seg), 2 * c)

    rep = rep_ref[...]
    kb = head_norm(proj(2048, 128), gkb_ref[...], seg[:PAIR, :PAIR]).astype(BF16)
    vb = proj(2176, 128).astype(BF16)
    write_pairs(kb_ref, jnp.dot(kb, rep, preferred_element_type=F32), 0)
    write_pairs(vb_ref, jnp.dot(vb, rep, preferred_element_type=F32), 0)


def _inproj(x2, n1, w_in, seg, rep, gqa, gka, gqb, gkb):
    n = x2.shape[0]
    t = TOKEN_TILE_IN
    const = lambda i: (0, 0)
    pair_out = lambda p: pl.BlockSpec((p, t, PAIR), lambda i: (0, i, 0))
    return pl.pallas_call(
        _inproj_kernel,
        grid=(n // t,),
        in_specs=[
            pl.BlockSpec((t, D_MODEL), lambda i: (i, 0)),
            pl.BlockSpec((1, D_MODEL), const),
            pl.BlockSpec((D_MODEL, IN_WIDTH), const),
            pl.BlockSpec((256, 256), const),
            pl.BlockSpec((PAIR, 256), const),
            pl.BlockSpec((1, 256), const),
            pl.BlockSpec((1, 256), const),
            pl.BlockSpec((1, 256), const),
            pl.BlockSpec((1, PAIR), const),
        ],
        out_specs=[pair_out(4), pair_out(4), pair_out(4), pair_out(4), pair_out(2), pair_out(2)],
        out_shape=[jax.ShapeDtypeStruct((p, n, PAIR), BF16) for p in (4, 4, 4, 4, 2, 2)],
        compiler_params=pltpu.CompilerParams(
            dimension_semantics=("arbitrary",), vmem_limit_bytes=VMEM_LIMIT),
        name="inproj",
    )(x2, n1, w_in, seg, rep, gqa, gka, gqb, gkb)


def _build_bias(bias_ref, cfg, slope0, slope1, radius, dist_scale):
    width = Q_BLOCK + 2 * radius
    row = lax.broadcasted_iota(jnp.int32, (Q_BLOCK, width), 0)
    col = lax.broadcasted_iota(jnp.int32, (Q_BLOCK, width), 1)
    rel = col - radius - row
    band = jnp.abs(rel) <= radius
    dist = jnp.abs(rel).astype(F32) * dist_scale
    for variant in range(4):
        ok = band
        if variant & 1:
            ok = ok & (col >= radius)
        if variant & 2:
            ok = ok & (col < Q_BLOCK + radius)
        base = jnp.where(ok, dist, MASKED)
        bias_ref[cfg, variant, 0:Q_BLOCK] = base * slope0
        bias_ref[cfg, variant, Q_BLOCK:2 * Q_BLOCK] = base * slope1


def _pair_scores(qblk, kwin, vwin, bias_ref, cfg, variant, sinks=None):
    lane = lax.broadcasted_iota(jnp.int32, (1, PAIR), 1)
    head0 = lane < HEAD_DIM
    q2 = jnp.concatenate([qblk * head0.astype(BF16), qblk * (~head0).astype(BF16)], axis=0)
    s = lax.dot_general(q2, kwin, (((1,), (1,)), ((), ())), preferred_element_type=F32)
    s = s - bias_ref[cfg, variant]
    m = jnp.max(s, axis=-1, keepdims=True)
    if sinks is not None:
        row = lax.broadcasted_iota(jnp.int32, (2 * Q_BLOCK, 1), 0)
        m = jnp.maximum(m, jnp.where(row < Q_BLOCK, sinks[0], sinks[1]))
    p = jnp.exp(s - m).astype(BF16)
    rhs = jnp.concatenate([vwin, jnp.ones_like(vwin)], axis=1)
    o = jnp.dot(p, rhs, preferred_element_type=F32)
    m_b = jnp.broadcast_to(m, (2 * Q_BLOCK, PAIR))
    pick = lambda a: jnp.where(head0, a[:Q_BLOCK], a[Q_BLOCK:])
    return pick(o[:, :PAIR]), pick(o[:, PAIR:]), pick(m_b)


def _fill_window(win_ref, prev_ref, cur_ref, next_ref, radius):
    rows = cur_ref.shape[0]
    win_ref[0:radius, :] = prev_ref[...]
    win_ref[radius:radius + rows, :] = cur_ref[...]
    win_ref[radius + rows:radius + rows + radius, :] = next_ref[...]


def _mixer_a_kernel(slopes_ref, *refs, n_tiles):
    nd = len(DILATIONS)
    q_refs = refs[0:nd]
    k_refs = refs[nd:4 * nd]
    v_refs = refs[4 * nd:7 * nd]
    o_ref = refs[7 * nd]
    scratch = refs[7 * nd + 1:]
    kw_refs = scratch[0:nd]
    vw_refs = scratch[nd:2 * nd]
    bias_ref, acc_ref, den_ref, max_ref = scratch[2 * nd:]

    hp = pl.program_id(0)
    b = pl.program_id(1)
    i = pl.program_id(2)
    t = TOKEN_TILE_ATTN
    slab = t // MERGE_STRIDE

    @pl.when((b == 0) & (i == 0))
    def _():
        for c, d in enumerate(DILATIONS):
            _build_bias(bias_ref, c, slopes_ref[2 * hp], slopes_ref[2 * hp + 1], RADIUS_A, float(d))

    for c, d in enumerate(DILATIONS):
        _fill_window(kw_refs[c], *k_refs[3 * c:3 * c + 3], RADIUS_A)
        _fill_window(vw_refs[c], *v_refs[3 * c:3 * c + 3], RADIUS_A)
        n_blocks = t // d // Q_BLOCK

        def block(qb, r, c=c, d=d, n_blocks=n_blocks):
            cols = slice(r * PAIR, (r + 1) * PAIR)
            q0 = qb * Q_BLOCK if n_blocks == 1 else pl.multiple_of(qb * Q_BLOCK, Q_BLOCK)
            first = (i == 0) & (qb == 0)
            last = (i == n_tiles - 1) & (qb == n_blocks - 1)
            variant = first.astype(jnp.int32) + 2 * last.astype(jnp.int32)
            qblk = q_refs[c][pl.ds(q0, Q_BLOCK), cols]
            kwin = kw_refs[c][pl.ds(q0, Q_BLOCK + 2 * RADIUS_A), cols]
            vwin = vw_refs[c][pl.ds(q0, Q_BLOCK + 2 * RADIUS_A), cols]
            acc, den, mx = _pair_scores(qblk, kwin, vwin, bias_ref, c, variant)
            if d == 1:
                rows = pl.ds(q0, Q_BLOCK)
            elif d == MERGE_STRIDE:
                rows = pl.ds(r * slab + q0, Q_BLOCK)
            else:
                sub = d // MERGE_STRIDE
                rows = pl.ds((r % MERGE_STRIDE) * slab + r // MERGE_STRIDE + sub * q0, Q_BLOCK, stride=sub)
            acc_ref[c, rows, :] = acc
            den_ref[c, rows, :] = den
            max_ref[c, rows, :] = mx

        def residues(qb, carry, d=d, block=block):
            for r in range(d):
                block(qb, r)
            return carry

        if n_blocks == 1:
            residues(0, 0)
        else:
            lax.fori_loop(0, n_blocks, residues, 0, unroll=max(1, BLOCKS_IN_FLIGHT // d))

    chunk = 256

    def merge(j, carry):
        i0 = pl.multiple_of(j * chunk, chunk)
        for r in range(MERGE_STRIDE):
            token_rows = pl.ds(r + MERGE_STRIDE * i0, chunk, stride=MERGE_STRIDE)
            slab_rows = pl.ds(r * slab + i0, chunk)
            rows = [token_rows if d == 1 else slab_rows for d in DILATIONS]
            ms = [max_ref[c, rows[c], :] for c in range(nd)]
            top = functools.reduce(jnp.maximum, ms)
            num = None
            den = None
            for c in range(nd):
                w = jnp.exp(ms[c] - top)
                n_c = w * acc_ref[c, rows[c], :]
                d_c = w * den_ref[c, rows[c], :]
                num = n_c if num is None else num + n_c
                den = d_c if den is None else den + d_c
            o_ref[token_rows, :] = num / den
        return carry

    lax.fori_loop(0, slab // chunk, merge, 0)


def _halo_specs(d, rows, total_rows, radius, lead_map):
    per = rows // radius
    last = total_rows // radius - 1
    width = d * PAIR

    def prev(*g):
        return (*lead_map(*g), jnp.maximum(g[-1] * per - 1, 0), 0)

    def cur(*g):
        return (*lead_map(*g), g[-1], 0)

    def nxt(*g):
        return (*lead_map(*g), jnp.minimum((g[-1] + 1) * per, last), 0)

    return [pl.BlockSpec((None, None, radius, width), prev),
            pl.BlockSpec((None, None, rows, width), cur),
            pl.BlockSpec((None, None, radius, width), nxt)]


def _mixer_a(qa, ka, va, slopes, batch, seq):
    t = TOKEN_TILE_ATTN
    n_tiles = seq // t
    lead = lambda hp, b, i: (hp, b)
    views = lambda a: [a.reshape(N_PAIRS, batch, seq // d, d * PAIR) for d in DILATIONS]
    in_specs = [pl.BlockSpec(memory_space=pltpu.SMEM)]
    for d in DILATIONS:
        in_specs.append(pl.BlockSpec((None, None, t // d, d * PAIR), lambda hp, b, i: (hp, b, i, 0)))
    halo = []
    for d in DILATIONS:
        halo += _halo_specs(d, t // d, seq // d, RADIUS_A, lead)
    in_specs += halo + halo
    k_views = [v for kv in views(ka) for v in (kv, kv, kv)]
    v_views = [v for vv in views(va) for v in (vv, vv, vv)]
    scratch = [pltpu.VMEM((t // d + 2 * RADIUS_A, d * PAIR), BF16) for d in DILATIONS] * 2
    scratch += [pltpu.VMEM((len(DILATIONS), 4, 2 * Q_BLOCK, Q_BLOCK + 2 * RADIUS_A), F32)]
    scratch += [pltpu.VMEM((len(DILATIONS), t, PAIR), F32)] * 3
    return pl.pallas_call(
        functools.partial(_mixer_a_kernel, n_tiles=n_tiles),
        grid=(N_PAIRS, batch, n_tiles),
        in_specs=in_specs,
        out_specs=pl.BlockSpec((None, t, PAIR), lambda hp, b, i: (b, i, hp)),
        out_shape=jax.ShapeDtypeStruct((batch, seq, WIDTH_A), F32),
        scratch_shapes=scratch,
        compiler_params=pltpu.CompilerParams(
            dimension_semantics=("arbitrary", "arbitrary", "arbitrary"), vmem_limit_bytes=VMEM_LIMIT),
        name="mixer_a",
    )(slopes, *views(qa), *k_views, *v_views)


def _mixer_b_kernel(slopes_ref, sink_ref, q_ref, kp_ref, kc_ref, kn_ref, vp_ref, vc_ref, vn_ref,
                    o_ref, kw_ref, vw_ref, bias_ref, *, n_tiles):
    hp = pl.program_id(0)
    b = pl.program_id(1)
    i = pl.program_id(2)
    t = TOKEN_TILE_ATTN
    n_blocks = t // Q_BLOCK

    @pl.when((b == 0) & (i == 0))
    def _():
        _build_bias(bias_ref, 0, slopes_ref[2 * hp], slopes_ref[2 * hp + 1], RADIUS_B, 1.0)

    _fill_window(kw_ref, kp_ref, kc_ref, kn_ref, RADIUS_B)
    _fill_window(vw_ref, vp_ref, vc_ref, vn_ref, RADIUS_B)
    sinks = (sink_ref[2 * hp], sink_ref[2 * hp + 1])
    lane = lax.broadcasted_iota(jnp.int32, (1, PAIR), 1)
    sink_pair = jnp.where(lane < HEAD_DIM, sinks[0], sinks[1])

    def block(qb, carry):
        q0 = pl.multiple_of(qb * Q_BLOCK, Q_BLOCK)
        first = (i == 0) & (qb == 0)
        last = (i == n_tiles - 1) & (qb == n_blocks - 1)
        variant = first.astype(jnp.int32) + 2 * last.astype(jnp.int32)
        qblk = q_ref[pl.ds(q0, Q_BLOCK), :]
        kwin = kw_ref[pl.ds(q0, Q_BLOCK + 2 * RADIUS_B), :]
        vwin = vw_ref[pl.ds(q0, Q_BLOCK + 2 * RADIUS_B), :]
        acc, den, mx = _pair_scores(qblk, kwin, vwin, bias_ref, 0, variant, sinks=sinks)
        den = den + jnp.exp(sink_pair - mx)
        o_ref[pl.ds(q0, Q_BLOCK), :] = acc / den
        return carry

    lax.fori_loop(0, n_blocks, block, 0, unroll=BLOCKS_IN_FLIGHT)


def _mixer_b(qb, kb, vb, slopes, sinks, batch, seq):
    t = TOKEN_TILE_ATTN
    n_tiles = seq // t
    kv_lead = lambda hp, b, i: (hp // 2, b)
    view = lambda a: a.reshape(a.shape[0], batch, seq, PAIR)
    halo = _halo_specs(1, t, seq, RADIUS_B, kv_lead)
    smem = pl.BlockSpec(memory_space=pltpu.SMEM)
    return pl.pallas_call(
        functools.partial(_mixer_b_kernel, n_tiles=n_tiles),
        grid=(N_PAIRS, batch, n_tiles),
        in_specs=[smem, smem, pl.BlockSpec((None, None, t, PAIR), lambda hp, b, i: (hp, b, i, 0))]
        + halo + halo,
        out_specs=pl.BlockSpec((None, t, PAIR), lambda hp, b, i: (b, i, hp)),
        out_shape=jax.ShapeDtypeStruct((batch, seq, WIDTH_B), F32),
        scratch_shapes=[pltpu.VMEM((t + 2 * RADIUS_B, PAIR), BF16)] * 2
        + [pltpu.VMEM((1, 4, 2 * Q_BLOCK, Q_BLOCK + 2 * RADIUS_B), F32)],
        compiler_params=pltpu.CompilerParams(
            dimension_semantics=("arbitrary", "arbitrary", "arbitrary"), vmem_limit_bytes=VMEM_LIMIT),
        name="mixer_b",
    )(slopes, sinks, view(qb), *([view(kb)] * 3), *([view(vb)] * 3))


def _rms(x, g):
    ms = jnp.mean(x * x, axis=-1, keepdims=True)
    return (x * lax.rsqrt(ms + EPS)) * g


def _outffn_kernel(yap_ref, yac_ref, yan_ref, ybp_ref, ybc_ref, ybn_ref, xp_ref, xc_ref, xn_ref,
                   ga_ref, gb_ref, wo_ref, n2_ref, wug_ref, wuv_ref, cwg_ref, cwv_ref, cbg_ref, cbv_ref,
                   wd_ref, o_ref, h2_ref, *, tiles_per_seq):
    i = pl.program_id(0)
    t = TOKEN_TILE_FFN
    halo = FFN_HALO
    first = (i % tiles_per_seq) == 0
    last = (i % tiles_per_seq) == tiles_per_seq - 1

    def mixed_rows(ya_ref, yb_ref, x_ref):
        na = _rms(ya_ref[...], ga_ref[...]).astype(BF16)
        nb = _rms(yb_ref[...], gb_ref[...]).astype(BF16)
        return (x_ref[...]
                + jnp.dot(na, wo_ref[0:WIDTH_A, :], preferred_element_type=F32)
                + jnp.dot(nb, wo_ref[WIDTH_A:, :], preferred_element_type=F32))

    h_prev = _rms(mixed_rows(yap_ref, ybp_ref, xp_ref), n2_ref[...])
    h2_ref[0:halo, :] = jnp.where(first, 0.0, h_prev).astype(BF16)
    h_next = _rms(mixed_rows(yan_ref, ybn_ref, xn_ref), n2_ref[...])
    h2_ref[halo + t:, :] = jnp.where(last, 0.0, h_next).astype(BF16)
    x1 = mixed_rows(yac_ref, ybc_ref, xc_ref)
    h2_ref[halo:halo + t, :] = _rms(x1, n2_ref[...]).astype(BF16)
    o_ref[...] = x1

    rows = t + 2 * halo

    def conv(u, w_ref, b_ref, j):
        w = w_ref[j]
        below = pltpu.roll(u, 1, axis=0)
        above = pltpu.roll(u, rows - 1, axis=0)
        c = b_ref[j] + below[halo:halo + t] * w[0:1]
        c = c + u[halo:halo + t] * w[1:2]
        return c + above[halo:halo + t] * w[2:3]

    def chunk(j, carry):
        h2 = h2_ref[...]
        gate = conv(jnp.dot(h2, wug_ref[j], preferred_element_type=F32), cwg_ref, cbg_ref, j)
        val = conv(jnp.dot(h2, wuv_ref[j], preferred_element_type=F32), cwv_ref, cbv_ref, j)
        hidden = (jax.nn.silu(gate) * val).astype(BF16)
        o_ref[...] += jnp.dot(hidden, wd_ref[j], preferred_element_type=F32)
        return carry

    lax.fori_loop(0, N_FF_CHUNKS, chunk, 0)


def _outffn(ya, yb, x2, ga, gb, w_out, n2, wug, wuv, cwg, cwv, cbg, cbv, wd, seq):
    n = x2.shape[0]
    t = TOKEN_TILE_FFN
    per = t // FFN_HALO
    last = n // FFN_HALO - 1
    prev = lambda i: (jnp.maximum(i * per - 1, 0), 0)
    cur = lambda i: (i, 0)
    nxt = lambda i: (jnp.minimum((i + 1) * per, last), 0)

    def halo_specs(width):
        return [pl.BlockSpec((FFN_HALO, width), prev), pl.BlockSpec((t, width), cur),
                pl.BlockSpec((FFN_HALO, width), nxt)]

    def resident(shape):
        return pl.BlockSpec(shape, lambda i: (0,) * len(shape), pipeline_mode=pl.Buffered(1))

    return pl.pallas_call(
        functools.partial(_outffn_kernel, tiles_per_seq=seq // t),
        grid=(n // t,),
        in_specs=halo_specs(WIDTH_A) + halo_specs(WIDTH_B) + halo_specs(D_MODEL) + [
            resident((1, WIDTH_A)), resident((1, WIDTH_B)), resident((D_MODEL, D_MODEL)),
            resident((1, D_MODEL)),
            resident((N_FF_CHUNKS, D_MODEL, FF_CHUNK)), resident((N_FF_CHUNKS, D_MODEL, FF_CHUNK)),
            resident((N_FF_CHUNKS, 3, FF_CHUNK)), resident((N_FF_CHUNKS, 3, FF_CHUNK)),
            resident((N_FF_CHUNKS, 1, FF_CHUNK)), resident((N_FF_CHUNKS, 1, FF_CHUNK)),
            resident((N_FF_CHUNKS, FF_CHUNK, D_MODEL)),
        ],
        out_specs=pl.BlockSpec((t, D_MODEL), cur),
        out_shape=jax.ShapeDtypeStruct((n, D_MODEL), F32),
        scratch_shapes=[pltpu.VMEM((t + 2 * FFN_HALO, D_MODEL), BF16)],
        compiler_params=pltpu.CompilerParams(
            dimension_semantics=("arbitrary",), vmem_limit_bytes=VMEM_LIMIT),
        name="outffn",
    )(ya, ya, ya, yb, yb, yb, x2, x2, x2, ga, gb, w_out, n2, wug, wuv, cwg, cwv, cbg, cbv, wd)


def _prepare(norm1, w_in, q_norm_a, k_norm_a, q_norm_b, k_norm_b, sink_b, out_norm_a, out_norm_b, w_out,
             norm2, w_up, conv_w, conv_b, w_down):
    scale = HEAD_DIM ** -0.5
    lanes = np.arange(256)
    seg = (lanes[:, None] // HEAD_DIM == lanes[None, :] // HEAD_DIM).astype(np.float32) / HEAD_DIM
    src = np.arange(PAIR)
    rep = ((src[:, None] // HEAD_DIM == lanes[None, :] // PAIR)
           & (src[:, None] % HEAD_DIM == lanes[None, :] % HEAD_DIM)).astype(np.float32)
    slopes_a, slopes_b = _alibi_slopes()

    def chunks(a):
        a = a.reshape(a.shape[:-1] + (N_FF_CHUNKS, FF_CHUNK))
        return jnp.moveaxis(a, -2, 0)

    return dict(
        n1=norm1.reshape(1, D_MODEL),
        w_in=w_in.astype(BF16),
        seg=jnp.asarray(seg, BF16),
        rep=jnp.asarray(rep, BF16),
        gqa=(jnp.tile(q_norm_a, 4) * scale).reshape(1, 256),
        gka=jnp.tile(k_norm_a, 4).reshape(1, 256),
        gqb=(jnp.tile(q_norm_b, 4) * scale).reshape(1, 256),
        gkb=jnp.tile(k_norm_b, 2).reshape(1, PAIR),
        slopes_a=jnp.asarray(slopes_a),
        slopes_b=jnp.asarray(slopes_b),
        sinks=sink_b.astype(F32),
        ga=out_norm_a.reshape(1, WIDTH_A),
        gb=out_norm_b.reshape(1, WIDTH_B),
        w_out=w_out.astype(BF16),
        n2=norm2.reshape(1, D_MODEL),
        wug=chunks(w_up[:, :D_FF]).astype(BF16),
        wuv=chunks(w_up[:, D_FF:]).astype(BF16),
        cwg=chunks(conv_w[:, :D_FF]),
        cwv=chunks(conv_w[:, D_FF:]),
        cbg=chunks(conv_b[:D_FF].reshape(1, D_FF)),
        cbv=chunks(conv_b[D_FF:].reshape(1, D_FF)),
        wd=w_down.reshape(N_FF_CHUNKS, FF_CHUNK, D_MODEL).astype(BF16),
    )


def _layer(x, p):
    batch, seq, _ = x.shape
    assert seq % TOKEN_TILE_ATTN == 0 and seq % TOKEN_TILE_FFN == 0
    x2 = x.reshape(batch * seq, D_MODEL)
    qa, ka, va, qb, kb, vb = _inproj(x2, p["n1"], p["w_in"], p["seg"], p["rep"],
                                     p["gqa"], p["gka"], p["gqb"], p["gkb"])
    ya = _mixer_a(qa, ka, va, p["slopes_a"], batch, seq).reshape(batch * seq, WIDTH_A)
    yb = _mixer_b(qb, kb, vb, p["slopes_b"], p["sinks"], batch, seq).reshape(batch * seq, WIDTH_B)
    out = _outffn(ya, yb, x2, p["ga"], p["gb"], p["w_out"], p["n2"], p["wug"], p["wuv"],
                  p["cwg"], p["cwv"], p["cbg"], p["cbv"], p["wd"], seq)
    return out.reshape(batch, seq, D_MODEL)


def kernel(x_prompt, x_sample, norm1, w_in, q_norm_a, k_norm_a, q_norm_b, k_norm_b, sink_b, out_norm_a,
           out_norm_b, w_out, norm2, w_up, conv_w, conv_b, w_down):
    y_prompt, y_sample = x_prompt, x_sample
    for l in range(norm1.shape[0]):
        p = _prepare(norm1[l], w_in[l], q_norm_a[l], k_norm_a[l], q_norm_b[l], k_norm_b[l], sink_b[l],
                     out_norm_a[l], out_norm_b[l], w_out[l], norm2[l], w_up[l], conv_w[l], conv_b[l],
                     w_down[l])
        y_prompt = _layer(y_prompt, p)
        y_sample = _layer(y_sample, p)
    return (y_prompt, y_sample)
```

```python
import functools

import numpy as np
import jax
import jax.numpy as jnp
from jax import lax
from jax.experimental import pallas as pl
from jax.experimental.pallas import tpu as pltpu

F32 = jnp.float32
BF16 = jnp.bfloat16

D_MODEL = 1024
HEAD_DIM = 64
PAIR = 2 * HEAD_DIM
N_PAIRS = 4
WIDTH_A = 512
WIDTH_B = 512
KV_WIDTH_B = 128
IN_WIDTH = 2304
DILATIONS = (1, 4, 16)
MERGE_STRIDE = 4
RADIUS_A = 64
RADIUS_B = 128
D_FF = 2816
EPS = 1e-6
MASKED = 1e30

Q_BLOCK = 128
BLOCKS_IN_FLIGHT = 8
TOKEN_TILE_IN = 512
TOKEN_TILE_ATTN = 2048
TOKEN_TILE_FFN = 512
FFN_HALO = 16
FF_CHUNK = 256
N_FF_CHUNKS = D_FF // FF_CHUNK
VMEM_LIMIT = 56 * 1024 * 1024


def _alibi_slopes():
    n = 16
    h = np.arange(1, n + 1, dtype=np.float32)
    s = np.exp2(-8.0 * h / n).astype(np.float32)
    return s[8:], s[:8]


def _inproj_kernel(x_ref, n1_ref, w_ref, seg_ref, rep_ref, gqa_ref, gka_ref, gqb_ref, gkb_ref,
                   qa1_ref, qa4_ref, qa16_ref, ka1_ref, ka4_ref, ka16_ref, va1_ref, va4_ref, va16_ref,
                   qb_ref, kb_ref, vb_ref, stage_ref):
    t = TOKEN_TILE_IN
    x = x_ref[...]
    ms = jnp.mean(x * x, axis=-1, keepdims=True)
    h = ((x * lax.rsqrt(ms + EPS)) * n1_ref[...]).astype(BF16)
    seg = seg_ref[...]

    def proj(c0, n):
        return jnp.dot(h, w_ref[:, c0:c0 + n], preferred_element_type=F32)

    def head_norm(p, g, seg_m):
        msq = jnp.dot((p * p).astype(BF16), seg_m, preferred_element_type=F32)
        return (p * lax.rsqrt(msq + EPS)) * g

    def write_pairs(ref, y, first_pair):
        for j in range(y.shape[1] // PAIR):
            ref[first_pair + j] = y[:, j * PAIR:(j + 1) * PAIR].astype(BF16)

    def write_dilated(refs, y, first_pair, first_slot):
        for j in range(y.shape[1] // PAIR):
            pair = first_pair + j
            yp = y[:, j * PAIR:(j + 1) * PAIR]
            refs[0][pair] = yp.astype(BF16)
            stage = stage_ref.at[first_slot + j]
            stage[...] = yp
            for ref, d in zip(refs[1:], DILATIONS[1:]):
                for r in range(d):
                    ref[pair, :, r * PAIR:(r + 1) * PAIR] = stage[pl.ds(r, t // d, stride=d), :].astype(BF16)

    qa_refs = (qa1_ref, qa4_ref, qa16_ref)
    ka_refs = (ka1_ref, ka4_ref, ka16_ref)
    va_refs = (va1_ref, va4_ref, va16_ref)
    def head_norm_wide(p, g):
        halves = [head_norm(p[:, c:c + 256], g, seg) for c in (0, 256)]
        return jnp.concatenate(halves, axis=1)

    write_dilated(qa_refs, head_norm_wide(proj(0, 512), gqa_ref[...]), 0, 0)
    write_dilated(ka_refs, head_norm_wide(proj(512, 512), gka_ref[...]), 0, 4)
    write_dilated(va_refs, proj(1024, 512), 0, 8)
    write_pairs(qb_ref, head_norm_wide(proj(1536, 512), gqb_ref[...]), 0)

    rep = rep_ref[...]
    kvb = proj(2048, 2 * KV_WIDTH_B)
    kb = head_norm(kvb[:, :KV_WIDTH_B], gkb_ref[...], seg[:PAIR, :PAIR]).astype(BF16)
    vb = kvb[:, KV_WIDTH_B:].astype(BF16)
    write_pairs(kb_ref, jnp.dot(kb, rep, preferred_element_type=F32), 0)
    write_pairs(vb_ref, jnp.dot(vb, rep, preferred_element_type=F32), 0)


def _inproj(x2, n1, w_in, seg, rep, gqa, gka, gqb, gkb):
    n = x2.shape[0]
    t = TOKEN_TILE_IN
    const = lambda i: (0, 0)
    pair_out = lambda p, d=1: pl.BlockSpec((p, t // d, d * PAIR), lambda i: (0, i, 0))
    pair_shape = lambda p, d=1: jax.ShapeDtypeStruct((p, n // d, d * PAIR), BF16)
    dilated = [(N_PAIRS, d) for _ in range(3) for d in DILATIONS]
    outs = dilated + [(N_PAIRS, 1), (2, 1), (2, 1)]
    res = pl.pallas_call(
        _inproj_kernel,
        grid=(n // t,),
        in_specs=[
            pl.BlockSpec((t, D_MODEL), lambda i: (i, 0)),
            pl.BlockSpec((1, D_MODEL), const),
            pl.BlockSpec((D_MODEL, IN_WIDTH), const),
            pl.BlockSpec((256, 256), const),
            pl.BlockSpec((PAIR, 256), const),
            pl.BlockSpec((1, 256), const),
            pl.BlockSpec((1, 256), const),
            pl.BlockSpec((1, 256), const),
            pl.BlockSpec((1, PAIR), const),
        ],
        out_specs=[pair_out(p, d) for p, d in outs],
        out_shape=[pair_shape(p, d) for p, d in outs],
        scratch_shapes=[pltpu.VMEM((3 * N_PAIRS, t, PAIR), F32)],
        compiler_params=pltpu.CompilerParams(
            dimension_semantics=("arbitrary",), vmem_limit_bytes=VMEM_LIMIT),
        name="inproj",
    )(x2, n1, w_in, seg, rep, gqa, gka, gqb, gkb)
    return res[0:3], res[3:6], res[6:9], res[9], res[10], res[11]


def _build_bias(bias_ref, cfg, slope0, slope1, radius, dist_scale):
    width = Q_BLOCK + 2 * radius
    row = lax.broadcasted_iota(jnp.int32, (Q_BLOCK, width), 0)
    col = lax.broadcasted_iota(jnp.int32, (Q_BLOCK, width), 1)
    rel = col - radius - row
    band = jnp.abs(rel) <= radius
    dist = jnp.abs(rel).astype(F32) * dist_scale
    for variant in range(4):
        ok = band
        if variant & 1:
            ok = ok & (col >= radius)
        if variant & 2:
            ok = ok & (col < Q_BLOCK + radius)
        base = jnp.where(ok, dist, MASKED)
        bias_ref[cfg, variant, 0:Q_BLOCK] = base * slope0
        bias_ref[cfg, variant, Q_BLOCK:2 * Q_BLOCK] = base * slope1


def _pair_scores(qblk, kwin, vwin, bias_ref, cfg, variant, sinks=None):
    lane = lax.broadcasted_iota(jnp.int32, (1, PAIR), 1)
    head0 = lane < HEAD_DIM
    q2 = jnp.concatenate([qblk * head0.astype(BF16), qblk * (~head0).astype(BF16)], axis=0)
    s = lax.dot_general(q2, kwin, (((1,), (1,)), ((), ())), preferred_element_type=F32)
    s = s - bias_ref[cfg, variant]
    m = jnp.max(s, axis=-1, keepdims=True)
    if sinks is not None:
        row = lax.broadcasted_iota(jnp.int32, (2 * Q_BLOCK, 1), 0)
        m = jnp.maximum(m, jnp.where(row < Q_BLOCK, sinks[0], sinks[1]))
    p = jnp.exp(s - m).astype(BF16)
    rhs = jnp.concatenate([vwin, jnp.ones_like(vwin)], axis=1)
    o = jnp.dot(p, rhs, preferred_element_type=F32)
    m_b = jnp.broadcast_to(m, (2 * Q_BLOCK, PAIR))
    pick = lambda a: jnp.where(head0, a[:Q_BLOCK], a[Q_BLOCK:])
    return pick(o[:, :PAIR]), pick(o[:, PAIR:]), pick(m_b)


def _fill_window(win_ref, prev_ref, cur_ref, next_ref, radius):
    rows = cur_ref.shape[0]
    win_ref[0:radius, :] = prev_ref[...]
    win_ref[radius:radius + rows, :] = cur_ref[...]
    win_ref[radius + rows:radius + rows + radius, :] = next_ref[...]


def _mixer_a_kernel(slopes_ref, *refs, n_tiles):
    nd = len(DILATIONS)
    q_refs = refs[0:nd]
    k_refs = refs[nd:4 * nd]
    v_refs = refs[4 * nd:7 * nd]
    o_ref = refs[7 * nd]
    scratch = refs[7 * nd + 1:]
    kw_refs = scratch[0:nd]
    vw_refs = scratch[nd:2 * nd]
    bias_ref, acc_ref, den_ref, max_ref = scratch[2 * nd:]

    hp = pl.program_id(0)
    b = pl.program_id(1)
    i = pl.program_id(2)
    t = TOKEN_TILE_ATTN
    slab = t // MERGE_STRIDE

    @pl.when((b == 0) & (i == 0))
    def _():
        for c, d in enumerate(DILATIONS):
            _build_bias(bias_ref, c, slopes_ref[2 * hp], slopes_ref[2 * hp + 1], RADIUS_A, float(d))

    for c, d in enumerate(DILATIONS):
        _fill_window(kw_refs[c], *k_refs[3 * c:3 * c + 3], RADIUS_A)
        _fill_window(vw_refs[c], *v_refs[3 * c:3 * c + 3], RADIUS_A)
        n_blocks = t // d // Q_BLOCK

        def block(qb, r, c=c, d=d, n_blocks=n_blocks):
            cols = slice(r * PAIR, (r + 1) * PAIR)
            q0 = qb * Q_BLOCK if n_blocks == 1 else pl.multiple_of(qb * Q_BLOCK, Q_BLOCK)
            first = (i == 0) & (qb == 0)
            last = (i == n_tiles - 1) & (qb == n_blocks - 1)
            variant = first.astype(jnp.int32) + 2 * last.astype(jnp.int32)
            qblk = q_refs[c][pl.ds(q0, Q_BLOCK), cols]
            kwin = kw_refs[c][pl.ds(q0, Q_BLOCK + 2 * RADIUS_A), cols]
            vwin = vw_refs[c][pl.ds(q0, Q_BLOCK + 2 * RADIUS_A), cols]
            acc, den, mx = _pair_scores(qblk, kwin, vwin, bias_ref, c, variant)
            if d == 1:
                rows = pl.ds(q0, Q_BLOCK)
            elif d == MERGE_STRIDE:
                rows = pl.ds(r * slab + q0, Q_BLOCK)
            else:
                sub = d // MERGE_STRIDE
                rows = pl.ds((r % MERGE_STRIDE) * slab + r // MERGE_STRIDE + sub * q0, Q_BLOCK, stride=sub)
            acc_ref[c, rows, :] = acc
            den_ref[c, rows, :] = den
            max_ref[c, rows, :] = mx

        def residues(qb, carry, d=d, block=block):
            for r in range(d):
                block(qb, r)
            return carry

        if n_blocks == 1:
            residues(0, 0)
        else:
            lax.fori_loop(0, n_blocks, residues, 0, unroll=max(1, BLOCKS_IN_FLIGHT // d))

    chunk = 256

    def merge(j, carry):
        i0 = pl.multiple_of(j * chunk, chunk)
        for r in range(MERGE_STRIDE):
            token_rows = pl.ds(r + MERGE_STRIDE * i0, chunk, stride=MERGE_STRIDE)
            slab_rows = pl.ds(r * slab + i0, chunk)
            rows = [token_rows if d == 1 else slab_rows for d in DILATIONS]
            ms = [max_ref[c, rows[c], :] for c in range(nd)]
            top = functools.reduce(jnp.maximum, ms)
            num = None
            den = None
            for c in range(nd):
                w = jnp.exp(ms[c] - top)
                n_c = w * acc_ref[c, rows[c], :]
                d_c = w * den_ref[c, rows[c], :]
                num = n_c if num is None else num + n_c
                den = d_c if den is None else den + d_c
            o_ref[token_rows, :] = num / den
        return carry

    lax.fori_loop(0, slab // chunk, merge, 0)


def _halo_specs(d, rows, total_rows, radius, lead_map):
    per = rows // radius
    last = total_rows // radius - 1
    width = d * PAIR

    def prev(*g):
        return (*lead_map(*g), jnp.maximum(g[-1] * per - 1, 0), 0)

    def cur(*g):
        return (*lead_map(*g), g[-1], 0)

    def nxt(*g):
        return (*lead_map(*g), jnp.minimum((g[-1] + 1) * per, last), 0)

    return [pl.BlockSpec((None, None, radius, width), prev),
            pl.BlockSpec((None, None, rows, width), cur),
            pl.BlockSpec((None, None, radius, width), nxt)]


def _mixer_a(qa, ka, va, slopes, batch, seq):
    t = TOKEN_TILE_ATTN
    n_tiles = seq // t
    lead = lambda hp, b, i: (hp, b)
    views = lambda a: [v.reshape(N_PAIRS, batch, seq // d, d * PAIR) for v, d in zip(a, DILATIONS)]
    in_specs = [pl.BlockSpec(memory_space=pltpu.SMEM)]
    for d in DILATIONS:
        in_specs.append(pl.BlockSpec((None, None, t // d, d * PAIR), lambda hp, b, i: (hp, b, i, 0)))
    halo = []
    for d in DILATIONS:
        halo += _halo_specs(d, t // d, seq // d, RADIUS_A, lead)
    in_specs += halo + halo
    k_views = [v for kv in views(ka) for v in (kv, kv, kv)]
    v_views = [v for vv in views(va) for v in (vv, vv, vv)]
    scratch = [pltpu.VMEM((t // d + 2 * RADIUS_A, d * PAIR), BF16) for d in DILATIONS] * 2
    scratch += [pltpu.VMEM((len(DILATIONS), 4, 2 * Q_BLOCK, Q_BLOCK + 2 * RADIUS_A), F32)]
    scratch += [pltpu.VMEM((len(DILATIONS), t, PAIR), F32)] * 3
    return pl.pallas_call(
        functools.partial(_mixer_a_kernel, n_tiles=n_tiles),
        grid=(N_PAIRS, batch, n_tiles),
        in_specs=in_specs,
        out_specs=pl.BlockSpec((None, t, PAIR), lambda hp, b, i: (b, i, hp)),
        out_shape=jax.ShapeDtypeStruct((batch, seq, WIDTH_A), F32),
        scratch_shapes=scratch,
        compiler_params=pltpu.CompilerParams(
            dimension_semantics=("arbitrary", "arbitrary", "arbitrary"), vmem_limit_bytes=VMEM_LIMIT),
        name="mixer_a",
    )(slopes, *views(qa), *k_views, *v_views)


def _mixer_b_kernel(slopes_ref, sink_ref, q_ref, kp_ref, kc_ref, kn_ref, vp_ref, vc_ref, vn_ref,
                    o_ref, kw_ref, vw_ref, bias_ref, *, n_tiles):
    hp = pl.program_id(0)
    b = pl.program_id(1)
    i = pl.program_id(2)
    t = TOKEN_TILE_ATTN
    n_blocks = t // Q_BLOCK

    @pl.when((b == 0) & (i == 0))
    def _():
        _build_bias(bias_ref, 0, slopes_ref[2 * hp], slopes_ref[2 * hp + 1], RADIUS_B, 1.0)

    _fill_window(kw_ref, kp_ref, kc_ref, kn_ref, RADIUS_B)
    _fill_window(vw_ref, vp_ref, vc_ref, vn_ref, RADIUS_B)
    sinks = (sink_ref[2 * hp], sink_ref[2 * hp + 1])
    lane = lax.broadcasted_iota(jnp.int32, (1, PAIR), 1)
    sink_pair = jnp.where(lane < HEAD_DIM, sinks[0], sinks[1])

    def block(qb, carry):
        q0 = pl.multiple_of(qb * Q_BLOCK, Q_BLOCK)
        first = (i == 0) & (qb == 0)
        last = (i == n_tiles - 1) & (qb == n_blocks - 1)
        variant = first.astype(jnp.int32) + 2 * last.astype(jnp.int32)
        qblk = q_ref[pl.ds(q0, Q_BLOCK), :]
        kwin = kw_ref[pl.ds(q0, Q_BLOCK + 2 * RADIUS_B), :]
        vwin = vw_ref[pl.ds(q0, Q_BLOCK + 2 * RADIUS_B), :]
        acc, den, mx = _pair_scores(qblk, kwin, vwin, bias_ref, 0, variant, sinks=sinks)
        den = den + jnp.exp(sink_pair - mx)
        o_ref[pl.ds(q0, Q_BLOCK), :] = acc / den
        return carry

    lax.fori_loop(0, n_blocks, block, 0, unroll=BLOCKS_IN_FLIGHT)


def _mixer_b(qb, kb, vb, slopes, sinks, batch, seq):
    t = TOKEN_TILE_ATTN
    n_tiles = seq // t
    kv_lead = lambda hp, b, i: (hp // 2, b)
    view = lambda a: a.reshape(a.shape[0], batch, seq, PAIR)
    halo = _halo_specs(1, t, seq, RADIUS_B, kv_lead)
    smem = pl.BlockSpec(memory_space=pltpu.SMEM)
    return pl.pallas_call(
        functools.partial(_mixer_b_kernel, n_tiles=n_tiles),
        grid=(N_PAIRS, batch, n_tiles),
        in_specs=[smem, smem, pl.BlockSpec((None, None, t, PAIR), lambda hp, b, i: (hp, b, i, 0))]
        + halo + halo,
        out_specs=pl.BlockSpec((None, t, PAIR), lambda hp, b, i: (b, i, hp)),
        out_shape=jax.ShapeDtypeStruct((batch, seq, WIDTH_B), F32),
        scratch_shapes=[pltpu.VMEM((t + 2 * RADIUS_B, PAIR), BF16)] * 2
        + [pltpu.VMEM((1, 4, 2 * Q_BLOCK, Q_BLOCK + 2 * RADIUS_B), F32)],
        compiler_params=pltpu.CompilerParams(
            dimension_semantics=("arbitrary", "arbitrary", "arbitrary"), vmem_limit_bytes=VMEM_LIMIT),
        name="mixer_b",
    )(slopes, sinks, view(qb), *([view(kb)] * 3), *([view(vb)] * 3))


def _rms(x, g):
    ms = jnp.mean(x * x, axis=-1, keepdims=True)
    return (x * lax.rsqrt(ms + EPS)) * g


def _outffn_kernel(yap_ref, yac_ref, yan_ref, ybp_ref, ybc_ref, ybn_ref, xp_ref, xc_ref, xn_ref,
                   ga_ref, gb_ref, wo_ref, n2_ref, wug_ref, wuv_ref, cwg_ref, cwv_ref, cbg_ref, cbv_ref,
                   wd_ref, o_ref, y_ref, h2_ref, u_ref, hid_ref, *, tiles_per_seq):
    i = pl.program_id(0)
    t = TOKEN_TILE_FFN
    halo = FFN_HALO
    first = (i % tiles_per_seq) == 0
    last = (i % tiles_per_seq) == tiles_per_seq - 1

    for r0, ya_ref, yb_ref in ((0, yap_ref, ybp_ref), (halo, yac_ref, ybc_ref), (halo + t, yan_ref, ybn_ref)):
        n = ya_ref.shape[0]
        y_ref[r0:r0 + n, 0:WIDTH_A] = _rms(ya_ref[...], ga_ref[...]).astype(BF16)
        y_ref[r0:r0 + n, WIDTH_A:] = _rms(yb_ref[...], gb_ref[...]).astype(BF16)
    mixed = jnp.dot(y_ref[...], wo_ref[...], preferred_element_type=F32)

    h_prev = _rms(xp_ref[...] + mixed[0:halo], n2_ref[...])
    h2_ref[0:halo, :] = jnp.where(first, 0.0, h_prev).astype(BF16)
    h_next = _rms(xn_ref[...] + mixed[halo + t:], n2_ref[...])
    h2_ref[halo + t:, :] = jnp.where(last, 0.0, h_next).astype(BF16)
    x1 = xc_ref[...] + mixed[halo:halo + t]
    h2_ref[halo:halo + t, :] = _rms(x1, n2_ref[...]).astype(BF16)
    o_ref[...] = x1

    def conv(j, which, w_up_ref, w_ref, b_ref):
        u = u_ref.at[j % 2, which]
        u[...] = jnp.dot(h2_ref[...], w_up_ref[j], preferred_element_type=F32)
        w = w_ref[j]
        c = b_ref[j] + u[pl.ds(halo - 1, t), :] * w[0:1]
        c = c + u[pl.ds(halo, t), :] * w[1:2]
        return c + u[pl.ds(halo + 1, t), :] * w[2:3]

    for j in range(N_FF_CHUNKS):
        gate = conv(j, 0, wug_ref, cwg_ref, cbg_ref)
        val = conv(j, 1, wuv_ref, cwv_ref, cbv_ref)
        hid_ref[:, j * FF_CHUNK:(j + 1) * FF_CHUNK] = (jax.nn.silu(gate) * val).astype(BF16)

    o_ref[...] += jnp.dot(hid_ref[...], wd_ref[...], preferred_element_type=F32)


def _outffn(ya, yb, x2, ga, gb, w_out, n2, wug, wuv, cwg, cwv, cbg, cbv, wd, seq):
    n = x2.shape[0]
    t = TOKEN_TILE_FFN
    per = t // FFN_HALO
    last = n // FFN_HALO - 1
    prev = lambda i: (jnp.maximum(i * per - 1, 0), 0)
    cur = lambda i: (i, 0)
    nxt = lambda i: (jnp.minimum((i + 1) * per, last), 0)

    def halo_specs(width):
        return [pl.BlockSpec((FFN_HALO, width), prev), pl.BlockSpec((t, width), cur),
                pl.BlockSpec((FFN_HALO, width), nxt)]

    def resident(shape):
        return pl.BlockSpec(shape, lambda i: (0,) * len(shape), pipeline_mode=pl.Buffered(1))

    return pl.pallas_call(
        functools.partial(_outffn_kernel, tiles_per_seq=seq // t),
        grid=(n // t,),
        in_specs=halo_specs(WIDTH_A) + halo_specs(WIDTH_B) + halo_specs(D_MODEL) + [
            resident((1, WIDTH_A)), resident((1, WIDTH_B)), resident((D_MODEL, D_MODEL)),
            resident((1, D_MODEL)),
            resident((N_FF_CHUNKS, D_MODEL, FF_CHUNK)), resident((N_FF_CHUNKS, D_MODEL, FF_CHUNK)),
            resident((N_FF_CHUNKS, 3, FF_CHUNK)), resident((N_FF_CHUNKS, 3, FF_CHUNK)),
            resident((N_FF_CHUNKS, 1, FF_CHUNK)), resident((N_FF_CHUNKS, 1, FF_CHUNK)),
            resident((D_FF, D_MODEL)),
        ],
        out_specs=pl.BlockSpec((t, D_MODEL), cur),
        out_shape=jax.ShapeDtypeStruct((n, D_MODEL), F32),
        scratch_shapes=[pltpu.VMEM((t + 2 * FFN_HALO, D_MODEL), BF16),
                        pltpu.VMEM((t + 2 * FFN_HALO, D_MODEL), BF16),
                        pltpu.VMEM((2, 2, t + 2 * FFN_HALO, FF_CHUNK), F32),
                        pltpu.VMEM((t, D_FF), BF16)],
        compiler_params=pltpu.CompilerParams(
            dimension_semantics=("arbitrary",), vmem_limit_bytes=VMEM_LIMIT),
        name="outffn",
    )(ya, ya, ya, yb, yb, yb, x2, x2, x2, ga, gb, w_out, n2, wug, wuv, cwg, cwv, cbg, cbv, wd)


def _prepare(norm1, w_in, q_norm_a, k_norm_a, q_norm_b, k_norm_b, sink_b, out_norm_a, out_norm_b, w_out,
             norm2, w_up, conv_w, conv_b, w_down):
    scale = HEAD_DIM ** -0.5
    lanes = np.arange(256)
    seg = (lanes[:, None] // HEAD_DIM == lanes[None, :] // HEAD_DIM).astype(np.float32) / HEAD_DIM
    src = np.arange(PAIR)
    rep = ((src[:, None] // HEAD_DIM == lanes[None, :] // PAIR)
           & (src[:, None] % HEAD_DIM == lanes[None, :] % HEAD_DIM)).astype(np.float32)
    slopes_a, slopes_b = _alibi_slopes()

    def chunks(a):
        a = a.reshape(a.shape[:-1] + (N_FF_CHUNKS, FF_CHUNK))
        return jnp.moveaxis(a, -2, 0)

    return dict(
        n1=norm1.reshape(1, D_MODEL),
        w_in=w_in.astype(BF16),
        seg=jnp.asarray(seg, BF16),
        rep=jnp.asarray(rep, BF16),
        gqa=(jnp.tile(q_norm_a, 4) * scale).reshape(1, 256),
        gka=jnp.tile(k_norm_a, 4).reshape(1, 256),
        gqb=(jnp.tile(q_norm_b, 4) * scale).reshape(1, 256),
        gkb=jnp.tile(k_norm_b, 2).reshape(1, PAIR),
        slopes_a=jnp.asarray(slopes_a),
        slopes_b=jnp.asarray(slopes_b),
        sinks=sink_b.astype(F32),
        ga=out_norm_a.reshape(1, WIDTH_A),
        gb=out_norm_b.reshape(1, WIDTH_B),
        w_out=w_out.astype(BF16),
        n2=norm2.reshape(1, D_MODEL),
        wug=chunks(w_up[:, :D_FF]).astype(BF16),
        wuv=chunks(w_up[:, D_FF:]).astype(BF16),
        cwg=chunks(conv_w[:, :D_FF]),
        cwv=chunks(conv_w[:, D_FF:]),
        cbg=chunks(conv_b[:D_FF].reshape(1, D_FF)),
        cbv=chunks(conv_b[D_FF:].reshape(1, D_FF)),
        wd=w_down.astype(BF16),
    )


def _layer(x, p):
    batch, seq, _ = x.shape
    assert seq % TOKEN_TILE_ATTN == 0 and seq % TOKEN_TILE_FFN == 0
    x2 = x.reshape(batch * seq, D_MODEL)
    qa, ka, va, qb, kb, vb = _inproj(x2, p["n1"], p["w_in"], p["seg"], p["rep"],
                                     p["gqa"], p["gka"], p["gqb"], p["gkb"])
    ya = _mixer_a(qa, ka, va, p["slopes_a"], batch, seq).reshape(batch * seq, WIDTH_A)
    yb = _mixer_b(qb, kb, vb, p["slopes_b"], p["sinks"], batch, seq).reshape(batch * seq, WIDTH_B)
    out = _outffn(ya, yb, x2, p["ga"], p["gb"], p["w_out"], p["n2"], p["wug"], p["wuv"],
                  p["cwg"], p["cwv"], p["cbg"], p["cbv"], p["wd"], seq)
    return out.reshape(batch, seq, D_MODEL)


def kernel(x_prompt, x_sample, norm1, w_in, q_norm_a, k_norm_a, q_norm_b, k_norm_b, sink_b, out_norm_a,
           out_norm_b, w_out, norm2, w_up, conv_w, conv_b, w_down):
    y_prompt, y_sample = x_prompt, x_sample
    for l in range(norm1.shape[0]):
        p = _prepare(norm1[l], w_in[l], q_norm_a[l], k_norm_a[l], q_norm_b[l], k_norm_b[l], sink_b[l],
                     out_norm_a[l], out_norm_b[l], w_out[l], norm2[l], w_up[l], conv_w[l], conv_b[l],
                     w_down[l])
        y_prompt = _layer(y_prompt, p)
        y_sample = _layer(y_sample, p)
    return (y_prompt, y_sample)
```

```python
import functools

import numpy as np
import jax
import jax.numpy as jnp
from jax import lax
from jax.experimental import pallas as pl
from jax.experimental.pallas import tpu as pltpu

F32 = jnp.float32
BF16 = jnp.bfloat16

D_MODEL = 1024
HEAD_DIM = 64
PAIR = 2 * HEAD_DIM
N_PAIRS = 4
WIDTH_A = 512
WIDTH_B = 512
KV_WIDTH_B = 128
IN_WIDTH = 2304
DILATIONS = (1, 4, 16)
MERGE_STRIDE = 4
RADIUS_A = 64
RADIUS_B = 128
D_FF = 2816
EPS = 1e-6
MASKED = 1e30

Q_BLOCK = 128
TOKEN_TILE_IN = 512
TOKEN_TILE_ATTN = 2048
TOKEN_TILE_FFN = 512
FFN_HALO = 16
FF_CHUNK = 256
N_FF_CHUNKS = D_FF // FF_CHUNK
VMEM_LIMIT = 56 * 1024 * 1024


def _alibi_slopes():
    n = 16
    h = np.arange(1, n + 1, dtype=np.float32)
    s = np.exp2(-8.0 * h / n).astype(np.float32)
    return s[8:], s[:8]


def _inproj_kernel(x_ref, n1_ref, w_ref, seg_ref, rep_ref, gqa_ref, gka_ref, gqb_ref, gkb_ref,
                   qa1_ref, qa4_ref, qa16_ref, ka1_ref, ka4_ref, ka16_ref, va1_ref, va4_ref, va16_ref,
                   qb_ref, kb_ref, vb_ref, stage_ref):
    t = TOKEN_TILE_IN
    x = x_ref[...]
    ms = jnp.mean(x * x, axis=-1, keepdims=True)
    h = ((x * lax.rsqrt(ms + EPS)) * n1_ref[...]).astype(BF16)
    seg = seg_ref[...]

    def proj(c0, n):
        return jnp.dot(h, w_ref[:, c0:c0 + n], preferred_element_type=F32)

    def head_norm(p, g, seg_m):
        msq = jnp.dot((p * p).astype(BF16), seg_m, preferred_element_type=F32)
        return (p * lax.rsqrt(msq + EPS)) * g

    def write_pairs(ref, y, first_pair):
        for j in range(y.shape[1] // PAIR):
            ref[first_pair + j] = y[:, j * PAIR:(j + 1) * PAIR].astype(BF16)

    def write_dilated(refs, y, first_pair, first_slot):
        for j in range(y.shape[1] // PAIR):
            pair = first_pair + j
            yp = y[:, j * PAIR:(j + 1) * PAIR]
            refs[0][pair] = yp.astype(BF16)
            stage = stage_ref.at[first_slot + j]
            stage[...] = yp
            for ref, d in zip(refs[1:], DILATIONS[1:]):
                for r in range(d):
                    ref[pair, :, r * PAIR:(r + 1) * PAIR] = stage[pl.ds(r, t // d, stride=d), :].astype(BF16)

    qa_refs = (qa1_ref, qa4_ref, qa16_ref)
    ka_refs = (ka1_ref, ka4_ref, ka16_ref)
    va_refs = (va1_ref, va4_ref, va16_ref)

    def head_norm_wide(p, g):
        halves = [head_norm(p[:, c:c + 256], g, seg) for c in (0, 256)]
        return jnp.concatenate(halves, axis=1)

    write_dilated(qa_refs, head_norm_wide(proj(0, 512), gqa_ref[...]), 0, 0)
    write_dilated(ka_refs, head_norm_wide(proj(512, 512), gka_ref[...]), 0, 4)
    write_dilated(va_refs, proj(1024, 512), 0, 8)
    write_pairs(qb_ref, head_norm_wide(proj(1536, 512), gqb_ref[...]), 0)

    rep = rep_ref[...]
    kvb = proj(2048, 2 * KV_WIDTH_B)
    kb = head_norm(kvb[:, :KV_WIDTH_B], gkb_ref[...], seg[:PAIR, :PAIR]).astype(BF16)
    vb = kvb[:, KV_WIDTH_B:].astype(BF16)
    write_pairs(kb_ref, jnp.dot(kb, rep, preferred_element_type=F32), 0)
    write_pairs(vb_ref, jnp.dot(vb, rep, preferred_element_type=F32), 0)


def _inproj(x2, n1, w_in, seg, rep, gqa, gka, gqb, gkb):
    n = x2.shape[0]
    t = TOKEN_TILE_IN
    const = lambda i: (0, 0)
    pair_out = lambda p, d=1: pl.BlockSpec((p, t // d, d * PAIR), lambda i: (0, i, 0))
    pair_shape = lambda p, d=1: jax.ShapeDtypeStruct((p, n // d, d * PAIR), BF16)
    dilated = [(N_PAIRS, d) for _ in range(3) for d in DILATIONS]
    outs = dilated + [(N_PAIRS, 1), (2, 1), (2, 1)]
    res = pl.pallas_call(
        _inproj_kernel,
        grid=(n // t,),
        in_specs=[
            pl.BlockSpec((t, D_MODEL), lambda i: (i, 0)),
            pl.BlockSpec((1, D_MODEL), const),
            pl.BlockSpec((D_MODEL, IN_WIDTH), const),
            pl.BlockSpec((256, 256), const),
            pl.BlockSpec((PAIR, 256), const),
            pl.BlockSpec((1, 256), const),
            pl.BlockSpec((1, 256), const),
            pl.BlockSpec((1, 256), const),
            pl.BlockSpec((1, PAIR), const),
        ],
        out_specs=[pair_out(p, d) for p, d in outs],
        out_shape=[pair_shape(p, d) for p, d in outs],
        scratch_shapes=[pltpu.VMEM((3 * N_PAIRS, t, PAIR), F32)],
        compiler_params=pltpu.CompilerParams(
            dimension_semantics=("arbitrary",), vmem_limit_bytes=VMEM_LIMIT),
        name="inproj",
    )(x2, n1, w_in, seg, rep, gqa, gka, gqb, gkb)
    return res[0:3], res[3:6], res[6:9], res[9], res[10], res[11]


def _build_bias(bias_ref, cfg, slopes, radius, dist_scale):
    width = Q_BLOCK + 2 * radius
    row = lax.broadcasted_iota(jnp.int32, (Q_BLOCK, width), 0)
    col = lax.broadcasted_iota(jnp.int32, (Q_BLOCK, width), 1)
    rel = col - radius - row
    band = jnp.abs(rel) <= radius
    dist = jnp.abs(rel).astype(F32) * dist_scale
    for variant in range(4):
        ok = band
        if variant & 1:
            ok = ok & (col >= radius)
        if variant & 2:
            ok = ok & (col < Q_BLOCK + radius)
        base = jnp.where(ok, dist, MASKED)
        for h, slope in enumerate(slopes):
            bias_ref[cfg, variant, h * Q_BLOCK:(h + 1) * Q_BLOCK] = base * slope


def _edge_variant(qb, n_blocks, i, n_tiles):
    variant = 0
    if qb == 0:
        variant = variant + (i == 0).astype(jnp.int32)
    if qb == n_blocks - 1:
        variant = variant + 2 * (i == n_tiles - 1).astype(jnp.int32)
    return variant


def _window(refs, q0, rows, cols, radius):
    prev_ref, cur_ref, next_ref = refs
    lo, hi = q0 - radius, q0 + Q_BLOCK + radius
    parts = []
    if lo < 0:
        parts.append(prev_ref[:, cols])
    parts.append(cur_ref[max(lo, 0):min(hi, rows), cols])
    if hi > rows:
        parts.append(next_ref[:, cols])
    return parts[0] if len(parts) == 1 else jnp.concatenate(parts, axis=0)


def _stacked_scores(qblks, kwin, vwin, bias_ref, cfg, variant, sinks=None):
    lane = lax.broadcasted_iota(jnp.int32, (1, PAIR), 1)
    head0 = lane < HEAD_DIM
    keep = (head0.astype(BF16), (~head0).astype(BF16))
    q = jnp.concatenate([qblk * k for qblk in qblks for k in keep], axis=0)
    s = lax.dot_general(q, kwin, (((1,), (1,)), ((), ())), preferred_element_type=F32)
    s = s - bias_ref[cfg, variant]
    m = jnp.max(s, axis=-1, keepdims=True)
    if sinks is not None:
        m = jnp.maximum(m, jnp.concatenate([jnp.full((Q_BLOCK, 1), sk, F32) for sk in sinks], axis=0))
    p = jnp.exp(s - m).astype(BF16)
    rhs = jnp.concatenate([vwin, jnp.ones_like(vwin)], axis=1)
    o = jnp.dot(p, rhs, preferred_element_type=F32)
    m_b = jnp.broadcast_to(m, (q.shape[0], PAIR))
    out = []
    for k in range(len(qblks)):
        top = slice(2 * k * Q_BLOCK, (2 * k + 1) * Q_BLOCK)
        bot = slice((2 * k + 1) * Q_BLOCK, (2 * k + 2) * Q_BLOCK)
        pick = lambda a, top=top, bot=bot: jnp.where(head0, a[top], a[bot])
        out.append((pick(o[:, :PAIR]), pick(o[:, PAIR:]), pick(m_b)))
    return out


def _mixer_a_kernel(slopes_ref, *refs, n_tiles):
    nd = len(DILATIONS)
    q_refs = refs[0:nd]
    k_refs = refs[nd:4 * nd]
    v_refs = refs[4 * nd:7 * nd]
    o_ref = refs[7 * nd]
    bias_ref, acc_ref, den_ref, max_ref = refs[7 * nd + 1:]

    hp = pl.program_id(0)
    b = pl.program_id(1)
    i = pl.program_id(2)
    t = TOKEN_TILE_ATTN
    slab = t // MERGE_STRIDE

    @pl.when((b == 0) & (i == 0))
    def _():
        for c, d in enumerate(DILATIONS):
            _build_bias(bias_ref, c, (slopes_ref[2 * hp], slopes_ref[2 * hp + 1]), RADIUS_A, float(d))

    for c, d in reversed(list(enumerate(DILATIONS))):
        rows_d = t // d
        n_blocks = rows_d // Q_BLOCK
        for qb in range(n_blocks):
            q0 = qb * Q_BLOCK
            variant = _edge_variant(qb, n_blocks, i, n_tiles)
            for r in range(d):
                cols = slice(r * PAIR, (r + 1) * PAIR)
                qblk = q_refs[c][q0:q0 + Q_BLOCK, cols]
                kwin = _window(k_refs[3 * c:3 * c + 3], q0, rows_d, cols, RADIUS_A)
                vwin = _window(v_refs[3 * c:3 * c + 3], q0, rows_d, cols, RADIUS_A)
                (acc, den, mx), = _stacked_scores([qblk], kwin, vwin, bias_ref, c, variant)
                if d == 1:
                    rows = pl.ds(q0, Q_BLOCK)
                elif d == MERGE_STRIDE:
                    rows = pl.ds(r * slab + q0, Q_BLOCK)
                else:
                    sub = d // MERGE_STRIDE
                    rows = pl.ds((r % MERGE_STRIDE) * slab + r // MERGE_STRIDE + sub * q0, Q_BLOCK,
                                 stride=sub)
                acc_ref[c, rows, :] = acc
                den_ref[c, rows, :] = den
                max_ref[c, rows, :] = mx

    chunk = 256
    for i0 in range(0, slab, chunk):
        for r in range(MERGE_STRIDE):
            token_rows = pl.ds(r + MERGE_STRIDE * i0, chunk, stride=MERGE_STRIDE)
            slab_rows = pl.ds(r * slab + i0, chunk)
            rows = [token_rows if d == 1 else slab_rows for d in DILATIONS]
            ms = [max_ref[c, rows[c], :] for c in range(nd)]
            top = functools.reduce(jnp.maximum, ms)
            num = None
            den = None
            for c in range(nd):
                w = jnp.exp(ms[c] - top)
                n_c = w * acc_ref[c, rows[c], :]
                d_c = w * den_ref[c, rows[c], :]
                num = n_c if num is None else num + n_c
                den = d_c if den is None else den + d_c
            o_ref[token_rows, :] = num / den


def _halo_specs(d, rows, total_rows, radius, lead_map):
    per = rows // radius
    last = total_rows // radius - 1
    width = d * PAIR

    def prev(*g):
        return (*lead_map(*g), jnp.maximum(g[-1] * per - 1, 0), 0)

    def cur(*g):
        return (*lead_map(*g), g[-1], 0)

    def nxt(*g):
        return (*lead_map(*g), jnp.minimum((g[-1] + 1) * per, last), 0)

    return [pl.BlockSpec((None, None, radius, width), prev),
            pl.BlockSpec((None, None, rows, width), cur),
            pl.BlockSpec((None, None, radius, width), nxt)]


def _mixer_a(qa, ka, va, slopes, batch, seq):
    t = TOKEN_TILE_ATTN
    n_tiles = seq // t
    lead = lambda hp, b, i: (hp, b)
    views = lambda a: [v.reshape(N_PAIRS, batch, seq // d, d * PAIR) for v, d in zip(a, DILATIONS)]
    in_specs = [pl.BlockSpec(memory_space=pltpu.SMEM)]
    for d in DILATIONS:
        in_specs.append(pl.BlockSpec((None, None, t // d, d * PAIR), lambda hp, b, i: (hp, b, i, 0)))
    halo = []
    for d in DILATIONS:
        halo += _halo_specs(d, t // d, seq // d, RADIUS_A, lead)
    in_specs += halo + halo
    k_views = [v for kv in views(ka) for v in (kv, kv, kv)]
    v_views = [v for vv in views(va) for v in (vv, vv, vv)]
    scratch = [pltpu.VMEM((len(DILATIONS), 4, 2 * Q_BLOCK, Q_BLOCK + 2 * RADIUS_A), F32)]
    scratch += [pltpu.VMEM((len(DILATIONS), t, PAIR), F32)] * 3
    return pl.pallas_call(
        functools.partial(_mixer_a_kernel, n_tiles=n_tiles),
        grid=(N_PAIRS, batch, n_tiles),
        in_specs=in_specs,
        out_specs=pl.BlockSpec((None, t, PAIR), lambda hp, b, i: (b, i, hp)),
        out_shape=jax.ShapeDtypeStruct((batch, seq, WIDTH_A), F32),
        scratch_shapes=scratch,
        compiler_params=pltpu.CompilerParams(
            dimension_semantics=("arbitrary", "arbitrary", "arbitrary"), vmem_limit_bytes=VMEM_LIMIT),
        name="mixer_a",
    )(slopes, *views(qa), *k_views, *v_views)


def _mixer_b_kernel(slopes_ref, sink_ref, q0_ref, q1_ref, kp_ref, kc_ref, kn_ref, vp_ref, vc_ref, vn_ref,
                    o_ref, bias_ref, *, n_tiles):
    g = pl.program_id(0)
    b = pl.program_id(1)
    i = pl.program_id(2)
    t = TOKEN_TILE_ATTN
    n_blocks = t // Q_BLOCK
    heads = [4 * g + h for h in range(4)]

    @pl.when((b == 0) & (i == 0))
    def _():
        _build_bias(bias_ref, 0, [slopes_ref[h] for h in heads], RADIUS_B, 1.0)

    sinks = [sink_ref[h] for h in heads]
    lane = lax.broadcasted_iota(jnp.int32, (1, PAIR), 1)
    every = slice(None)
    for qb in range(n_blocks):
        q0 = qb * Q_BLOCK
        variant = _edge_variant(qb, n_blocks, i, n_tiles)
        qblks = [q_ref[q0:q0 + Q_BLOCK, :] for q_ref in (q0_ref, q1_ref)]
        kwin = _window((kp_ref, kc_ref, kn_ref), q0, t, every, RADIUS_B)
        vwin = _window((vp_ref, vc_ref, vn_ref), q0, t, every, RADIUS_B)
        stats = _stacked_scores(qblks, kwin, vwin, bias_ref, 0, variant, sinks=sinks)
        for k, (acc, den, mx) in enumerate(stats):
            sink_pair = jnp.where(lane < HEAD_DIM, sinks[2 * k], sinks[2 * k + 1])
            den = den + jnp.exp(sink_pair - mx)
            o_ref[q0:q0 + Q_BLOCK, k * PAIR:(k + 1) * PAIR] = acc / den


def _mixer_b(qb, kb, vb, slopes, sinks, batch, seq):
    t = TOKEN_TILE_ATTN
    n_tiles = seq // t
    kv_lead = lambda g, b, i: (g, b)
    view = lambda a: a.reshape(a.shape[0], batch, seq, PAIR)
    halo = _halo_specs(1, t, seq, RADIUS_B, kv_lead)
    smem = pl.BlockSpec(memory_space=pltpu.SMEM)
    q_spec = lambda k: pl.BlockSpec((None, None, t, PAIR), lambda g, b, i: (2 * g + k, b, i, 0))
    return pl.pallas_call(
        functools.partial(_mixer_b_kernel, n_tiles=n_tiles),
        grid=(2, batch, n_tiles),
        in_specs=[smem, smem, q_spec(0), q_spec(1)] + halo + halo,
        out_specs=pl.BlockSpec((None, t, 2 * PAIR), lambda g, b, i: (b, i, g)),
        out_shape=jax.ShapeDtypeStruct((batch, seq, WIDTH_B), F32),
        scratch_shapes=[pltpu.VMEM((1, 4, 4 * Q_BLOCK, Q_BLOCK + 2 * RADIUS_B), F32)],
        compiler_params=pltpu.CompilerParams(
            dimension_semantics=("arbitrary", "arbitrary", "arbitrary"), vmem_limit_bytes=VMEM_LIMIT),
        name="mixer_b",
    )(slopes, sinks, view(qb), view(qb), *([view(kb)] * 3), *([view(vb)] * 3))


def _rms(x, g):
    ms = jnp.mean(x * x, axis=-1, keepdims=True)
    return (x * lax.rsqrt(ms + EPS)) * g


def _outffn_kernel(yap_ref, yac_ref, yan_ref, ybp_ref, ybc_ref, ybn_ref, xp_ref, xc_ref, xn_ref,
                   ga_ref, gb_ref, wo_ref, n2_ref, wug_ref, wuv_ref, cwg_ref, cwv_ref, cbg_ref, cbv_ref,
                   wd_ref, o_ref, y_ref, h2_ref, u_ref, hid_ref, *, tiles_per_seq):
    i = pl.program_id(0)
    t = TOKEN_TILE_FFN
    halo = FFN_HALO
    first = (i % tiles_per_seq) == 0
    last = (i % tiles_per_seq) == tiles_per_seq - 1

    for r0, ya_ref, yb_ref in ((0, yap_ref, ybp_ref), (halo, yac_ref, ybc_ref), (halo + t, yan_ref, ybn_ref)):
        n = ya_ref.shape[0]
        y_ref[r0:r0 + n, 0:WIDTH_A] = _rms(ya_ref[...], ga_ref[...]).astype(BF16)
        y_ref[r0:r0 + n, WIDTH_A:] = _rms(yb_ref[...], gb_ref[...]).astype(BF16)
    mixed = jnp.dot(y_ref[...], wo_ref[...], preferred_element_type=F32)

    h_prev = _rms(xp_ref[...] + mixed[0:halo], n2_ref[...])
    h2_ref[0:halo, :] = jnp.where(first, 0.0, h_prev).astype(BF16)
    h_next = _rms(xn_ref[...] + mixed[halo + t:], n2_ref[...])
    h2_ref[halo + t:, :] = jnp.where(last, 0.0, h_next).astype(BF16)
    x1 = xc_ref[...] + mixed[halo:halo + t]
    h2_ref[halo:halo + t, :] = _rms(x1, n2_ref[...]).astype(BF16)
    o_ref[...] = x1

    def conv(j, which, w_up_ref, w_ref, b_ref):
        u = u_ref.at[j % 2, which]
        u[...] = jnp.dot(h2_ref[...], w_up_ref[j], preferred_element_type=F32)
        w = w_ref[j]
        c = b_ref[j] + u[pl.ds(halo - 1, t), :] * w[0:1]
        c = c + u[pl.ds(halo, t), :] * w[1:2]
        return c + u[pl.ds(halo + 1, t), :] * w[2:3]

    for j in range(N_FF_CHUNKS):
        gate = conv(j, 0, wug_ref, cwg_ref, cbg_ref)
        val = conv(j, 1, wuv_ref, cwv_ref, cbv_ref)
        hid_ref[:, j * FF_CHUNK:(j + 1) * FF_CHUNK] = (jax.nn.silu(gate) * val).astype(BF16)

    o_ref[...] += jnp.dot(hid_ref[...], wd_ref[...], preferred_element_type=F32)


def _outffn(ya, yb, x2, ga, gb, w_out, n2, wug, wuv, cwg, cwv, cbg, cbv, wd, seq):
    n = x2.shape[0]
    t = TOKEN_TILE_FFN
    per = t // FFN_HALO
    last = n // FFN_HALO - 1
    prev = lambda i: (jnp.maximum(i * per - 1, 0), 0)
    cur = lambda i: (i, 0)
    nxt = lambda i: (jnp.minimum((i + 1) * per, last), 0)

    def halo_specs(width):
        return [pl.BlockSpec((FFN_HALO, width), prev), pl.BlockSpec((t, width), cur),
                pl.BlockSpec((FFN_HALO, width), nxt)]

    def resident(shape):
        return pl.BlockSpec(shape, lambda i: (0,) * len(shape), pipeline_mode=pl.Buffered(1))

    return pl.pallas_call(
        functools.partial(_outffn_kernel, tiles_per_seq=seq // t),
        grid=(n // t,),
        in_specs=halo_specs(WIDTH_A) + halo_specs(WIDTH_B) + halo_specs(D_MODEL) + [
            resident((1, WIDTH_A)), resident((1, WIDTH_B)), resident((D_MODEL, D_MODEL)),
            resident((1, D_MODEL)),
            resident((N_FF_CHUNKS, D_MODEL, FF_CHUNK)), resident((N_FF_CHUNKS, D_MODEL, FF_CHUNK)),
            resident((N_FF_CHUNKS, 3, FF_CHUNK)), resident((N_FF_CHUNKS, 3, FF_CHUNK)),
            resident((N_FF_CHUNKS, 1, FF_CHUNK)), resident((N_FF_CHUNKS, 1, FF_CHUNK)),
            resident((D_FF, D_MODEL)),
        ],
        out_specs=pl.BlockSpec((t, D_MODEL), cur),
        out_shape=jax.ShapeDtypeStruct((n, D_MODEL), F32),
        scratch_shapes=[pltpu.VMEM((t + 2 * FFN_HALO, D_MODEL), BF16),
                        pltpu.VMEM((t + 2 * FFN_HALO, D_MODEL), BF16),
                        pltpu.VMEM((2, 2, t + 2 * FFN_HALO, FF_CHUNK), F32),
                        pltpu.VMEM((t, D_FF), BF16)],
        compiler_params=pltpu.CompilerParams(
            dimension_semantics=("arbitrary",), vmem_limit_bytes=VMEM_LIMIT),
        name="outffn",
    )(ya, ya, ya, yb, yb, yb, x2, x2, x2, ga, gb, w_out, n2, wug, wuv, cwg, cwv, cbg, cbv, wd)


def _prepare(norm1, w_in, q_norm_a, k_norm_a, q_norm_b, k_norm_b, sink_b, out_norm_a, out_norm_b, w_out,
             norm2, w_up, conv_w, conv_b, w_down):
    scale = HEAD_DIM ** -0.5
    lanes = np.arange(256)
    seg = (lanes[:, None] // HEAD_DIM == lanes[None, :] // HEAD_DIM).astype(np.float32) / HEAD_DIM
    src = np.arange(PAIR)
    rep = ((src[:, None] // HEAD_DIM == lanes[None, :] // PAIR)
           & (src[:, None] % HEAD_DIM == lanes[None, :] % HEAD_DIM)).astype(np.float32)
    slopes_a, slopes_b = _alibi_slopes()

    def chunks(a):
        a = a.reshape(a.shape[:-1] + (N_FF_CHUNKS, FF_CHUNK))
        return jnp.moveaxis(a, -2, 0)

    return dict(
        n1=norm1.reshape(1, D_MODEL),
        w_in=w_in.astype(BF16),
        seg=jnp.asarray(seg, BF16),
        rep=jnp.asarray(rep, BF16),
        gqa=(jnp.tile(q_norm_a, 4) * scale).reshape(1, 256),
        gka=jnp.tile(k_norm_a, 4).reshape(1, 256),
        gqb=(jnp.tile(q_norm_b, 4) * scale).reshape(1, 256),
        gkb=jnp.tile(k_norm_b, 2).reshape(1, PAIR),
        slopes_a=jnp.asarray(slopes_a),
        slopes_b=jnp.asarray(slopes_b),
        sinks=sink_b.astype(F32),
        ga=out_norm_a.reshape(1, WIDTH_A),
        gb=out_norm_b.reshape(1, WIDTH_B),
        w_out=w_out.astype(BF16),
        n2=norm2.reshape(1, D_MODEL),
        wug=chunks(w_up[:, :D_FF]).astype(BF16),
        wuv=chunks(w_up[:, D_FF:]).astype(BF16),
        cwg=chunks(conv_w[:, :D_FF]),
        cwv=chunks(conv_w[:, D_FF:]),
        cbg=chunks(conv_b[:D_FF].reshape(1, D_FF)),
        cbv=chunks(conv_b[D_FF:].reshape(1, D_FF)),
        wd=w_down.astype(BF16),
    )


def _layer(x, p):
    batch, seq, _ = x.shape
    assert seq % TOKEN_TILE_ATTN == 0 and seq % TOKEN_TILE_FFN == 0
    x2 = x.reshape(batch * seq, D_MODEL)
    qa, ka, va, qb, kb, vb = _inproj(x2, p["n1"], p["w_in"], p["seg"], p["rep"],
                                     p["gqa"], p["gka"], p["gqb"], p["gkb"])
    ya = _mixer_a(qa, ka, va, p["slopes_a"], batch, seq).reshape(batch * seq, WIDTH_A)
    yb = _mixer_b(qb, kb, vb, p["slopes_b"], p["sinks"], batch, seq).reshape(batch * seq, WIDTH_B)
    out = _outffn(ya, yb, x2, p["ga"], p["gb"], p["w_out"], p["n2"], p["wug"], p["wuv"],
                  p["cwg"], p["cwv"], p["cbg"], p["cbv"], p["wd"], seq)
    return out.reshape(batch, seq, D_MODEL)


def kernel(x_prompt, x_sample, norm1, w_in, q_norm_a, k_norm_a, q_norm_b, k_norm_b, sink_b, out_norm_a,
           out_norm_b, w_out, norm2, w_up, conv_w, conv_b, w_down):
    y_prompt, y_sample = x_prompt, x_sample
    for l in range(norm1.shape[0]):
        p = _prepare(norm1[l], w_in[l], q_norm_a[l], k_norm_a[l], q_norm_b[l], k_norm_b[l], sink_b[l],
                     out_norm_a[l], out_norm_b[l], w_out[l], norm2[l], w_up[l], conv_w[l], conv_b[l],
                     w_down[l])
        y_prompt = _layer(y_prompt, p)
        y_sample = _layer(y_sample, p)
    return (y_prompt, y_sample)
```

```python
import functools

import numpy as np
import jax
import jax.numpy as jnp
from jax import lax
from jax.experimental import pallas as pl
from jax.experimental.pallas import tpu as pltpu

F32 = jnp.float32
BF16 = jnp.bfloat16

D_MODEL = 1024
HEAD_DIM = 64
PAIR = 2 * HEAD_DIM
N_PAIRS = 4
WIDTH_A = 512
WIDTH_B = 512
KV_WIDTH_B = 128
IN_WIDTH = 2304
DILATIONS = (1, 4, 16)
MERGE_STRIDE = 4
RADIUS_A = 64
RADIUS_B = 128
D_FF = 2816
EPS = 1e-6
MASKED = 1e30

Q_BLOCK = 128
TOKEN_TILE_IN = 512
TOKEN_TILE_ATTN = 2048
TOKEN_TILE_FFN = 512
FFN_HALO = 16
FF_CHUNK = 256
N_FF_CHUNKS = D_FF // FF_CHUNK
VMEM_LIMIT = 56 * 1024 * 1024


def _alibi_slopes():
    n = 16
    h = np.arange(1, n + 1, dtype=np.float32)
    s = np.exp2(-8.0 * h / n).astype(np.float32)
    return s[8:], s[:8]


def _inproj_kernel(x_ref, n1_ref, w_ref, seg_ref, rep_ref, gqa_ref, gka_ref, gqb_ref, gkb_ref,
                   qa1_ref, qa4_ref, qa16_ref, ka1_ref, ka4_ref, ka16_ref, va1_ref, va4_ref, va16_ref,
                   qb_ref, kb_ref, vb_ref, stage_ref):
    t = TOKEN_TILE_IN
    x = x_ref[...]
    ms = jnp.mean(x * x, axis=-1, keepdims=True)
    h = ((x * lax.rsqrt(ms + EPS)) * n1_ref[...]).astype(BF16)
    seg = seg_ref[...]

    def proj(c0, n):
        return jnp.dot(h, w_ref[:, c0:c0 + n], preferred_element_type=F32)

    def head_norm(p, g, seg_m):
        msq = jnp.dot((p * p).astype(BF16), seg_m, preferred_element_type=F32)
        return (p * lax.rsqrt(msq + EPS)) * g

    def write_pairs(ref, y, first_pair):
        for j in range(y.shape[1] // PAIR):
            ref[first_pair + j] = y[:, j * PAIR:(j + 1) * PAIR].astype(BF16)

    def write_dilated(refs, y, first_pair, first_slot):
        for j in range(y.shape[1] // PAIR):
            pair = first_pair + j
            yp = y[:, j * PAIR:(j + 1) * PAIR]
            refs[0][pair] = yp.astype(BF16)
            stage = stage_ref.at[first_slot + j]
            stage[...] = yp
            for ref, d in zip(refs[1:], DILATIONS[1:]):
                for r in range(d):
                    ref[pair, :, r * PAIR:(r + 1) * PAIR] = stage[pl.ds(r, t // d, stride=d), :].astype(BF16)

    qa_refs = (qa1_ref, qa4_ref, qa16_ref)
    ka_refs = (ka1_ref, ka4_ref, ka16_ref)
    va_refs = (va1_ref, va4_ref, va16_ref)

    def head_norm_wide(p, g):
        halves = [head_norm(p[:, c:c + 256], g, seg) for c in (0, 256)]
        return jnp.concatenate(halves, axis=1)

    write_dilated(qa_refs, head_norm_wide(proj(0, 512), gqa_ref[...]), 0, 0)
    write_dilated(ka_refs, head_norm_wide(proj(512, 512), gka_ref[...]), 0, 4)
    write_dilated(va_refs, proj(1024, 512), 0, 8)
    write_pairs(qb_ref, head_norm_wide(proj(1536, 512), gqb_ref[...]), 0)

    rep = rep_ref[...]
    kvb = proj(2048, 2 * KV_WIDTH_B)
    kb = head_norm(kvb[:, :KV_WIDTH_B], gkb_ref[...], seg[:PAIR, :PAIR]).astype(BF16)
    vb = kvb[:, KV_WIDTH_B:].astype(BF16)
    write_pairs(kb_ref, jnp.dot(kb, rep, preferred_element_type=F32), 0)
    write_pairs(vb_ref, jnp.dot(vb, rep, preferred_element_type=F32), 0)


def _inproj(x2, n1, w_in, seg, rep, gqa, gka, gqb, gkb):
    n = x2.shape[0]
    t = TOKEN_TILE_IN
    const = lambda i: (0, 0)
    pair_out = lambda p, d=1: pl.BlockSpec((p, t // d, d * PAIR), lambda i: (0, i, 0))
    pair_shape = lambda p, d=1: jax.ShapeDtypeStruct((p, n // d, d * PAIR), BF16)
    dilated = [(N_PAIRS, d) for _ in range(3) for d in DILATIONS]
    outs = dilated + [(N_PAIRS, 1), (2, 1), (2, 1)]
    res = pl.pallas_call(
        _inproj_kernel,
        grid=(n // t,),
        in_specs=[
            pl.BlockSpec((t, D_MODEL), lambda i: (i, 0)),
            pl.BlockSpec((1, D_MODEL), const),
            pl.BlockSpec((D_MODEL, IN_WIDTH), const),
            pl.BlockSpec((256, 256), const),
            pl.BlockSpec((PAIR, 256), const),
            pl.BlockSpec((1, 256), const),
            pl.BlockSpec((1, 256), const),
            pl.BlockSpec((1, 256), const),
            pl.BlockSpec((1, PAIR), const),
        ],
        out_specs=[pair_out(p, d) for p, d in outs],
        out_shape=[pair_shape(p, d) for p, d in outs],
        scratch_shapes=[pltpu.VMEM((3 * N_PAIRS, t, PAIR), F32)],
        compiler_params=pltpu.CompilerParams(
            dimension_semantics=("arbitrary",), vmem_limit_bytes=VMEM_LIMIT),
        name="inproj",
    )(x2, n1, w_in, seg, rep, gqa, gka, gqb, gkb)
    return res[0:3], res[3:6], res[6:9], res[9], res[10], res[11]


def _build_bias(bias_ref, cfg, slopes, radius, dist_scale):
    width = Q_BLOCK + 2 * radius
    row = lax.broadcasted_iota(jnp.int32, (Q_BLOCK, width), 0)
    col = lax.broadcasted_iota(jnp.int32, (Q_BLOCK, width), 1)
    rel = col - radius - row
    band = jnp.abs(rel) <= radius
    dist = jnp.abs(rel).astype(F32) * dist_scale
    for variant in range(4):
        ok = band
        if variant & 1:
            ok = ok & (col >= radius)
        if variant & 2:
            ok = ok & (col < Q_BLOCK + radius)
        base = jnp.where(ok, dist, MASKED)
        for h, slope in enumerate(slopes):
            bias_ref[cfg, variant, h * Q_BLOCK:(h + 1) * Q_BLOCK] = base * slope


def _edge_variant(qb, n_blocks, i, n_tiles):
    variant = 0
    if qb == 0:
        variant = variant + (i == 0).astype(jnp.int32)
    if qb == n_blocks - 1:
        variant = variant + 2 * (i == n_tiles - 1).astype(jnp.int32)
    return variant


def _window(refs, q0, rows, cols, radius):
    prev_ref, cur_ref, next_ref = refs
    lo, hi = q0 - radius, q0 + Q_BLOCK + radius
    parts = []
    if lo < 0:
        parts.append(prev_ref[:, cols])
    parts.append(cur_ref[max(lo, 0):min(hi, rows), cols])
    if hi > rows:
        parts.append(next_ref[:, cols])
    return parts[0] if len(parts) == 1 else jnp.concatenate(parts, axis=0)


def _stacked_scores(qblks, kwin, vwin, bias_ref, cfg, variant, sinks=None):
    lane = lax.broadcasted_iota(jnp.int32, (1, PAIR), 1)
    head0 = lane < HEAD_DIM
    keep = (head0.astype(BF16), (~head0).astype(BF16))
    q = jnp.concatenate([qblk * k for qblk in qblks for k in keep], axis=0)
    s = lax.dot_general(q, kwin, (((1,), (1,)), ((), ())), preferred_element_type=F32)
    s = s - bias_ref[cfg, variant]
    m = jnp.max(s, axis=-1, keepdims=True)
    if sinks is not None:
        m = jnp.maximum(m, jnp.concatenate([jnp.full((Q_BLOCK, 1), sk, F32) for sk in sinks], axis=0))
    p = jnp.exp(s - m).astype(BF16)
    rhs = jnp.concatenate([vwin, jnp.ones_like(vwin)], axis=1)
    o = jnp.dot(p, rhs, preferred_element_type=F32)
    m_b = jnp.broadcast_to(m, (q.shape[0], PAIR))
    out = []
    for k in range(len(qblks)):
        top = slice(2 * k * Q_BLOCK, (2 * k + 1) * Q_BLOCK)
        bot = slice((2 * k + 1) * Q_BLOCK, (2 * k + 2) * Q_BLOCK)
        pick = lambda a, top=top, bot=bot: jnp.where(head0, a[top], a[bot])
        out.append((pick(o[:, :PAIR]), pick(o[:, PAIR:]), pick(m_b)))
    return out


def _mixer_a_kernel(slopes_ref, *refs, n_tiles):
    nd = len(DILATIONS)
    q_refs = refs[0:nd]
    k_refs = refs[nd:4 * nd]
    v_refs = refs[4 * nd:7 * nd]
    o_ref = refs[7 * nd]
    bias_ref, acc_ref, den_ref, max_ref = refs[7 * nd + 1:]

    hp = pl.program_id(0)
    b = pl.program_id(1)
    i = pl.program_id(2)
    t = TOKEN_TILE_ATTN
    slab = t // MERGE_STRIDE

    @pl.when((b == 0) & (i == 0))
    def _():
        for c, d in enumerate(DILATIONS):
            _build_bias(bias_ref, c, (slopes_ref[2 * hp], slopes_ref[2 * hp + 1]), RADIUS_A, float(d))

    for c, d in reversed(list(enumerate(DILATIONS))):
        rows_d = t // d
        n_blocks = rows_d // Q_BLOCK
        for qb in range(n_blocks):
            q0 = qb * Q_BLOCK
            variant = _edge_variant(qb, n_blocks, i, n_tiles)
            for r in range(d):
                cols = slice(r * PAIR, (r + 1) * PAIR)
                qblk = q_refs[c][q0:q0 + Q_BLOCK, cols]
                kwin = _window(k_refs[3 * c:3 * c + 3], q0, rows_d, cols, RADIUS_A)
                vwin = _window(v_refs[3 * c:3 * c + 3], q0, rows_d, cols, RADIUS_A)
                (acc, den, mx), = _stacked_scores([qblk], kwin, vwin, bias_ref, c, variant)
                if d == 1:
                    rows = pl.ds(q0, Q_BLOCK)
                elif d == MERGE_STRIDE:
                    rows = pl.ds(r * slab + q0, Q_BLOCK)
                else:
                    sub = d // MERGE_STRIDE
                    rows = pl.ds((r % MERGE_STRIDE) * slab + r // MERGE_STRIDE + sub * q0, Q_BLOCK,
                                 stride=sub)
                acc_ref[c, rows, :] = acc
                den_ref[c, rows, :] = den
                max_ref[c, rows, :] = mx

    chunk = 256
    for i0 in range(0, slab, chunk):
        for r in range(MERGE_STRIDE):
            token_rows = pl.ds(r + MERGE_STRIDE * i0, chunk, stride=MERGE_STRIDE)
            slab_rows = pl.ds(r * slab + i0, chunk)
            rows = [token_rows if d == 1 else slab_rows for d in DILATIONS]
            ms = [max_ref[c, rows[c], :] for c in range(nd)]
            top = functools.reduce(jnp.maximum, ms)
            num = None
            den = None
            for c in range(nd):
                w = jnp.exp(ms[c] - top)
                n_c = w * acc_ref[c, rows[c], :]
                d_c = w * den_ref[c, rows[c], :]
                num = n_c if num is None else num + n_c
                den = d_c if den is None else den + d_c
            o_ref[token_rows, :] = num / den


def _halo_specs(d, rows, total_rows, radius, lead_map):
    per = rows // radius
    last = total_rows // radius - 1
    width = d * PAIR

    def prev(*g):
        return (*lead_map(*g), jnp.maximum(g[-1] * per - 1, 0), 0)

    def cur(*g):
        return (*lead_map(*g), g[-1], 0)

    def nxt(*g):
        return (*lead_map(*g), jnp.minimum((g[-1] + 1) * per, last), 0)

    return [pl.BlockSpec((None, None, radius, width), prev),
            pl.BlockSpec((None, None, rows, width), cur),
            pl.BlockSpec((None, None, radius, width), nxt)]


def _mixer_a(qa, ka, va, slopes, batch, seq):
    t = TOKEN_TILE_ATTN
    n_tiles = seq // t
    lead = lambda hp, b, i: (hp, b)
    views = lambda a: [v.reshape(N_PAIRS, batch, seq // d, d * PAIR) for v, d in zip(a, DILATIONS)]
    in_specs = [pl.BlockSpec(memory_space=pltpu.SMEM)]
    for d in DILATIONS:
        in_specs.append(pl.BlockSpec((None, None, t // d, d * PAIR), lambda hp, b, i: (hp, b, i, 0)))
    halo = []
    for d in DILATIONS:
        halo += _halo_specs(d, t // d, seq // d, RADIUS_A, lead)
    in_specs += halo + halo
    k_views = [v for kv in views(ka) for v in (kv, kv, kv)]
    v_views = [v for vv in views(va) for v in (vv, vv, vv)]
    scratch = [pltpu.VMEM((len(DILATIONS), 4, 2 * Q_BLOCK, Q_BLOCK + 2 * RADIUS_A), F32)]
    scratch += [pltpu.VMEM((len(DILATIONS), t, PAIR), F32)] * 3
    return pl.pallas_call(
        functools.partial(_mixer_a_kernel, n_tiles=n_tiles),
        grid=(N_PAIRS, batch, n_tiles),
        in_specs=in_specs,
        out_specs=pl.BlockSpec((None, t, PAIR), lambda hp, b, i: (b, i, hp)),
        out_shape=jax.ShapeDtypeStruct((batch, seq, WIDTH_A), F32),
        scratch_shapes=scratch,
        compiler_params=pltpu.CompilerParams(
            dimension_semantics=("arbitrary", "arbitrary", "arbitrary"), vmem_limit_bytes=VMEM_LIMIT),
        name="mixer_a",
    )(slopes, *views(qa), *k_views, *v_views)


def _mixer_b_kernel(slopes_ref, sink_ref, q0_ref, q1_ref, kp_ref, kc_ref, kn_ref, vp_ref, vc_ref, vn_ref,
                    o_ref, bias_ref, *, n_tiles):
    g = pl.program_id(0)
    b = pl.program_id(1)
    i = pl.program_id(2)
    t = TOKEN_TILE_ATTN
    n_blocks = t // Q_BLOCK
    heads = [4 * g + h for h in range(4)]

    @pl.when((b == 0) & (i == 0))
    def _():
        _build_bias(bias_ref, 0, [slopes_ref[h] for h in heads], RADIUS_B, 1.0)

    sinks = [sink_ref[h] for h in heads]
    lane = lax.broadcasted_iota(jnp.int32, (1, PAIR), 1)
    every = slice(None)
    for qb in range(n_blocks):
        q0 = qb * Q_BLOCK
        variant = _edge_variant(qb, n_blocks, i, n_tiles)
        qblks = [q_ref[q0:q0 + Q_BLOCK, :] for q_ref in (q0_ref, q1_ref)]
        kwin = _window((kp_ref, kc_ref, kn_ref), q0, t, every, RADIUS_B)
        vwin = _window((vp_ref, vc_ref, vn_ref), q0, t, every, RADIUS_B)
        stats = _stacked_scores(qblks, kwin, vwin, bias_ref, 0, variant, sinks=sinks)
        for k, (acc, den, mx) in enumerate(stats):
            sink_pair = jnp.where(lane < HEAD_DIM, sinks[2 * k], sinks[2 * k + 1])
            den = den + jnp.exp(sink_pair - mx)
            o_ref[q0:q0 + Q_BLOCK, k * PAIR:(k + 1) * PAIR] = acc / den


def _mixer_b(qb, kb, vb, slopes, sinks, batch, seq):
    t = TOKEN_TILE_ATTN
    n_tiles = seq // t
    kv_lead = lambda g, b, i: (g, b)
    view = lambda a: a.reshape(a.shape[0], batch, seq, PAIR)
    halo = _halo_specs(1, t, seq, RADIUS_B, kv_lead)
    smem = pl.BlockSpec(memory_space=pltpu.SMEM)
    q_spec = lambda k: pl.BlockSpec((None, None, t, PAIR), lambda g, b, i: (2 * g + k, b, i, 0))
    return pl.pallas_call(
        functools.partial(_mixer_b_kernel, n_tiles=n_tiles),
        grid=(2, batch, n_tiles),
        in_specs=[smem, smem, q_spec(0), q_spec(1)] + halo + halo,
        out_specs=pl.BlockSpec((None, t, 2 * PAIR), lambda g, b, i: (b, i, g)),
        out_shape=jax.ShapeDtypeStruct((batch, seq, WIDTH_B), F32),
        scratch_shapes=[pltpu.VMEM((1, 4, 4 * Q_BLOCK, Q_BLOCK + 2 * RADIUS_B), F32)],
        compiler_params=pltpu.CompilerParams(
            dimension_semantics=("arbitrary", "arbitrary", "arbitrary"), vmem_limit_bytes=VMEM_LIMIT),
        name="mixer_b",
    )(slopes, sinks, view(qb), view(qb), *([view(kb)] * 3), *([view(vb)] * 3))


def _rms(x, g):
    ms = jnp.mean(x * x, axis=-1, keepdims=True)
    return (x * lax.rsqrt(ms + EPS)) * g


def _outffn_kernel(yap_ref, yac_ref, yan_ref, ybp_ref, ybc_ref, ybn_ref, xp_ref, xc_ref, xn_ref,
                   ga_ref, gb_ref, wo_ref, n2_ref, wu_ref, cw_ref, cb_ref,
                   wd_ref, o_ref, y_ref, h2_ref, perm_ref, hid_ref, *, tiles_per_seq):
    i = pl.program_id(0)
    t = TOKEN_TILE_FFN
    halo = FFN_HALO
    first = (i % tiles_per_seq) == 0
    last = (i % tiles_per_seq) == tiles_per_seq - 1

    for r0, ya_ref, yb_ref in ((0, yap_ref, ybp_ref), (halo, yac_ref, ybc_ref), (halo + t, yan_ref, ybn_ref)):
        n = ya_ref.shape[0]
        y_ref[r0:r0 + n, 0:WIDTH_A] = _rms(ya_ref[...], ga_ref[...]).astype(BF16)
        y_ref[r0:r0 + n, WIDTH_A:] = _rms(yb_ref[...], gb_ref[...]).astype(BF16)
    mixed = jnp.dot(y_ref[...], wo_ref[...], preferred_element_type=F32)

    h_prev = jnp.where(first, 0.0, _rms(xp_ref[...] + mixed[0:halo], n2_ref[...]))
    h_next = jnp.where(last, 0.0, _rms(xn_ref[...] + mixed[halo + t:], n2_ref[...]))
    x1 = xc_ref[...] + mixed[halo:halo + t]
    o_ref[...] = x1

    per = t // 8
    h_cur = _rms(x1, n2_ref[...])
    lane_tiles = D_MODEL // 128
    for c in range(lane_tiles):
        for s in range(8):
            perm_ref[c, pl.ds(s, per, stride=8), :] = h_cur[s * per:(s + 1) * per, c * 128:(c + 1) * 128]
        h2_ref[0:t, c * 128:(c + 1) * 128] = perm_ref[c].astype(BF16)
    h2_ref[t:, :] = jnp.concatenate([h_prev[halo - 8:], h_next[:8]], axis=0).astype(BF16)

    sub = lax.broadcasted_iota(jnp.int32, (8, FF_CHUNK), 0)

    def conv(u, w, b):
        body, edge = u[0:t], u[t:]
        lo = jnp.where(sub == 0, pltpu.roll(edge[0:8], 1, axis=0), pltpu.roll(body[t - 8:], 1, axis=0))
        hi = jnp.where(sub == 7, pltpu.roll(edge[8:16], 7, axis=0), pltpu.roll(body[0:8], 7, axis=0))
        below = jnp.concatenate([lo, body[:t - 8]], axis=0)
        above = jnp.concatenate([body[8:], hi], axis=0)
        return b + below * w[0:1] + body * w[1:2] + above * w[2:3]

    for j in range(N_FF_CHUNKS):
        h2 = h2_ref[...]
        branch = []
        for c0 in (j * FF_CHUNK, D_FF + j * FF_CHUNK):
            cols = slice(c0, c0 + FF_CHUNK)
            u = jnp.dot(h2, wu_ref[:, cols], preferred_element_type=F32)
            branch.append(conv(u, cw_ref[:, cols], cb_ref[:, cols]))
        gate, val = branch
        hid_ref[:, j * FF_CHUNK:(j + 1) * FF_CHUNK] = (jax.nn.silu(gate) * val).astype(BF16)

    down = jnp.dot(hid_ref[...], wd_ref[...], preferred_element_type=F32)
    for c in range(lane_tiles):
        perm_ref[c] = down[:, c * 128:(c + 1) * 128]
        for s in range(8):
            o_ref[s * per:(s + 1) * per, c * 128:(c + 1) * 128] += perm_ref[c, pl.ds(s, per, stride=8), :]


def _outffn(ya, yb, x2, ga, gb, w_out, n2, wu, cw, cb, wd, seq):
    n = x2.shape[0]
    t = TOKEN_TILE_FFN
    per = t // FFN_HALO
    last = n // FFN_HALO - 1
    prev = lambda i: (jnp.maximum(i * per - 1, 0), 0)
    cur = lambda i: (i, 0)
    nxt = lambda i: (jnp.minimum((i + 1) * per, last), 0)

    def halo_specs(width):
        return [pl.BlockSpec((FFN_HALO, width), prev), pl.BlockSpec((t, width), cur),
                pl.BlockSpec((FFN_HALO, width), nxt)]

    def resident(shape):
        return pl.BlockSpec(shape, lambda i: (0,) * len(shape), pipeline_mode=pl.Buffered(1))

    return pl.pallas_call(
        functools.partial(_outffn_kernel, tiles_per_seq=seq // t),
        grid=(n // t,),
        in_specs=halo_specs(WIDTH_A) + halo_specs(WIDTH_B) + halo_specs(D_MODEL) + [
            resident((1, WIDTH_A)), resident((1, WIDTH_B)), resident((D_MODEL, D_MODEL)),
            resident((1, D_MODEL)),
            resident((D_MODEL, 2 * D_FF)), resident((3, 2 * D_FF)), resident((1, 2 * D_FF)),
            resident((D_FF, D_MODEL)),
        ],
        out_specs=pl.BlockSpec((t, D_MODEL), cur),
        out_shape=jax.ShapeDtypeStruct((n, D_MODEL), F32),
        scratch_shapes=[pltpu.VMEM((t + 2 * FFN_HALO, D_MODEL), BF16),
                        pltpu.VMEM((t + FFN_HALO, D_MODEL), BF16),
                        pltpu.VMEM((D_MODEL // 128, t, 128), F32),
                        pltpu.VMEM((t, D_FF), BF16)],
        compiler_params=pltpu.CompilerParams(
            dimension_semantics=("arbitrary",), vmem_limit_bytes=VMEM_LIMIT),
        name="outffn",
    )(ya, ya, ya, yb, yb, yb, x2, x2, x2, ga, gb, w_out, n2, wu, cw, cb, wd)


def _prepare(norm1, w_in, q_norm_a, k_norm_a, q_norm_b, k_norm_b, sink_b, out_norm_a, out_norm_b, w_out,
             norm2, w_up, conv_w, conv_b, w_down):
    scale = HEAD_DIM ** -0.5
    lanes = np.arange(256)
    seg = (lanes[:, None] // HEAD_DIM == lanes[None, :] // HEAD_DIM).astype(np.float32) / HEAD_DIM
    src = np.arange(PAIR)
    rep = ((src[:, None] // HEAD_DIM == lanes[None, :] // PAIR)
           & (src[:, None] % HEAD_DIM == lanes[None, :] % HEAD_DIM)).astype(np.float32)
    slopes_a, slopes_b = _alibi_slopes()

    return dict(
        n1=norm1.reshape(1, D_MODEL),
        w_in=w_in.astype(BF16),
        seg=jnp.asarray(seg, BF16),
        rep=jnp.asarray(rep, BF16),
        gqa=(jnp.tile(q_norm_a, 4) * scale).reshape(1, 256),
        gka=jnp.tile(k_norm_a, 4).reshape(1, 256),
        gqb=(jnp.tile(q_norm_b, 4) * scale).reshape(1, 256),
        gkb=jnp.tile(k_norm_b, 2).reshape(1, PAIR),
        slopes_a=jnp.asarray(slopes_a),
        slopes_b=jnp.asarray(slopes_b),
        sinks=sink_b.astype(F32),
        ga=out_norm_a.reshape(1, WIDTH_A),
        gb=out_norm_b.reshape(1, WIDTH_B),
        w_out=w_out.astype(BF16),
        n2=norm2.reshape(1, D_MODEL),
        wu=w_up.astype(BF16),
        cw=conv_w,
        cb=conv_b.reshape(1, 2 * D_FF),
        wd=w_down.astype(BF16),
    )


def _layer(x, p):
    batch, seq, _ = x.shape
    assert seq % TOKEN_TILE_ATTN == 0 and seq % TOKEN_TILE_FFN == 0
    x2 = x.reshape(batch * seq, D_MODEL)
    qa, ka, va, qb, kb, vb = _inproj(x2, p["n1"], p["w_in"], p["seg"], p["rep"],
                                     p["gqa"], p["gka"], p["gqb"], p["gkb"])
    ya = _mixer_a(qa, ka, va, p["slopes_a"], batch, seq).reshape(batch * seq, WIDTH_A)
    yb = _mixer_b(qb, kb, vb, p["slopes_b"], p["sinks"], batch, seq).reshape(batch * seq, WIDTH_B)
    out = _outffn(ya, yb, x2, p["ga"], p["gb"], p["w_out"], p["n2"], p["wu"], p["cw"], p["cb"],
                  p["wd"], seq)
    return out.reshape(batch, seq, D_MODEL)


def kernel(x_prompt, x_sample, norm1, w_in, q_norm_a, k_norm_a, q_norm_b, k_norm_b, sink_b, out_norm_a,
           out_norm_b, w_out, norm2, w_up, conv_w, conv_b, w_down):
    y_prompt, y_sample = x_prompt, x_sample
    for l in range(norm1.shape[0]):
        p = _prepare(norm1[l], w_in[l], q_norm_a[l], k_norm_a[l], q_norm_b[l], k_norm_b[l], sink_b[l],
                     out_norm_a[l], out_norm_b[l], w_out[l], norm2[l], w_up[l], conv_w[l], conv_b[l],
                     w_down[l])
        y_prompt = _layer(y_prompt, p)
        y_sample = _layer(y_sample, p)
    return (y_prompt, y_sample)
```

```python
import functools

import numpy as np
import jax
import jax.numpy as jnp
from jax import lax
from jax.experimental import pallas as pl
from jax.experimental.pallas import tpu as pltpu

F32 = jnp.float32
BF16 = jnp.bfloat16

D_MODEL = 1024
HEAD_DIM = 64
PAIR = 2 * HEAD_DIM
N_PAIRS = 4
WIDTH_A = 512
WIDTH_B = 512
KV_WIDTH_B = 128
IN_WIDTH = 2304
DILATIONS = (1, 4, 16)
MERGE_STRIDE = 4
RADIUS_A = 64
RADIUS_B = 128
D_FF = 2816
EPS = 1e-6
MASKED = 1e30
LOG2E = 1.4426950408889634

Q_BLOCK = 128
SCORE_AHEAD = 2
TOKEN_TILE_IN = 512
TOKEN_TILE_ATTN = 2048
TOKEN_TILE_FFN = 512
FFN_HALO = 16
FF_CHUNK = 256
N_FF_CHUNKS = D_FF // FF_CHUNK
VMEM_LIMIT = 56 * 1024 * 1024


def _alibi_slopes():
    n = 16
    h = np.arange(1, n + 1, dtype=np.float32)
    s = np.exp2(-8.0 * h / n).astype(np.float32)
    return s[8:], s[:8]


def _inproj_kernel(x_ref, n1_ref, w_ref, seg_ref, rep_ref, gqa_ref, gka_ref, gqb_ref, gkb_ref,
                   qa1_ref, qa4_ref, qa16_ref, ka1_ref, ka4_ref, ka16_ref, va1_ref, va4_ref, va16_ref,
                   qb_ref, kb_ref, vb_ref, stage_ref):
    t = TOKEN_TILE_IN
    x = x_ref[...]
    ms = jnp.mean(x * x, axis=-1, keepdims=True)
    h = ((x * lax.rsqrt(ms + EPS)) * n1_ref[...]).astype(BF16)
    seg = seg_ref[...]

    def proj(c0, n):
        return jnp.dot(h, w_ref[:, c0:c0 + n], preferred_element_type=F32)

    def head_norm(p, g, seg_m):
        msq = jnp.dot((p * p).astype(BF16), seg_m, preferred_element_type=F32)
        return (p * lax.rsqrt(msq + EPS)) * g

    def write_pairs(ref, y, first_pair):
        for j in range(y.shape[1] // PAIR):
            ref[first_pair + j] = y[:, j * PAIR:(j + 1) * PAIR].astype(BF16)

    def write_dilated(refs, y, first_pair, first_slot):
        for j in range(y.shape[1] // PAIR):
            pair = first_pair + j
            yp = y[:, j * PAIR:(j + 1) * PAIR]
            refs[0][pair] = yp.astype(BF16)
            stage = stage_ref.at[first_slot + j]
            stage[...] = yp
            for ref, d in zip(refs[1:], DILATIONS[1:]):
                for r in range(d):
                    ref[pair, :, r * PAIR:(r + 1) * PAIR] = stage[pl.ds(r, t // d, stride=d), :].astype(BF16)

    qa_refs = (qa1_ref, qa4_ref, qa16_ref)
    ka_refs = (ka1_ref, ka4_ref, ka16_ref)
    va_refs = (va1_ref, va4_ref, va16_ref)

    def head_norm_wide(p, g):
        halves = [head_norm(p[:, c:c + 256], g, seg) for c in (0, 256)]
        return jnp.concatenate(halves, axis=1)

    write_dilated(qa_refs, head_norm_wide(proj(0, 512), gqa_ref[...]), 0, 0)
    write_dilated(ka_refs, head_norm_wide(proj(512, 512), gka_ref[...]), 0, 4)
    write_dilated(va_refs, proj(1024, 512), 0, 8)
    write_pairs(qb_ref, head_norm_wide(proj(1536, 512), gqb_ref[...]), 0)

    rep = rep_ref[...]
    kvb = proj(2048, 2 * KV_WIDTH_B)
    kb = head_norm(kvb[:, :KV_WIDTH_B], gkb_ref[...], seg[:PAIR, :PAIR]).astype(BF16)
    vb = kvb[:, KV_WIDTH_B:].astype(BF16)
    write_pairs(kb_ref, jnp.dot(kb, rep, preferred_element_type=F32), 0)
    write_pairs(vb_ref, jnp.dot(vb, rep, preferred_element_type=F32), 0)


def _inproj(x2, n1, w_in, seg, rep, gqa, gka, gqb, gkb):
    n = x2.shape[0]
    t = TOKEN_TILE_IN
    const = lambda i: (0, 0)
    pair_out = lambda p, d=1: pl.BlockSpec((p, t // d, d * PAIR), lambda i: (0, i, 0))
    pair_shape = lambda p, d=1: jax.ShapeDtypeStruct((p, n // d, d * PAIR), BF16)
    dilated = [(N_PAIRS, d) for _ in range(3) for d in DILATIONS]
    outs = dilated + [(N_PAIRS, 1), (2, 1), (2, 1)]
    res = pl.pallas_call(
        _inproj_kernel,
        grid=(n // t,),
        in_specs=[
            pl.BlockSpec((t, D_MODEL), lambda i: (i, 0)),
            pl.BlockSpec((1, D_MODEL), const),
            pl.BlockSpec((D_MODEL, IN_WIDTH), const),
            pl.BlockSpec((256, 256), const),
            pl.BlockSpec((PAIR, 256), const),
            pl.BlockSpec((1, 256), const),
            pl.BlockSpec((1, 256), const),
            pl.BlockSpec((1, 256), const),
            pl.BlockSpec((1, PAIR), const),
        ],
        out_specs=[pair_out(p, d) for p, d in outs],
        out_shape=[pair_shape(p, d) for p, d in outs],
        scratch_shapes=[pltpu.VMEM((3 * N_PAIRS, t, PAIR), F32)],
        compiler_params=pltpu.CompilerParams(
            dimension_semantics=("arbitrary",), vmem_limit_bytes=VMEM_LIMIT),
        name="inproj",
    )(x2, n1, w_in, seg, rep, gqa, gka, gqb, gkb)
    return res[0:3], res[3:6], res[6:9], res[9], res[10], res[11]


def _build_bias(bias_ref, cfg, slopes, radius, dist_scale):
    width = Q_BLOCK + 2 * radius
    row = lax.broadcasted_iota(jnp.int32, (Q_BLOCK, width), 0)
    col = lax.broadcasted_iota(jnp.int32, (Q_BLOCK, width), 1)
    rel = col - radius - row
    band = jnp.abs(rel) <= radius
    dist = jnp.abs(rel).astype(F32) * dist_scale
    for variant in range(4):
        ok = band
        if variant & 1:
            ok = ok & (col >= radius)
        if variant & 2:
            ok = ok & (col < Q_BLOCK + radius)
        base = jnp.where(ok, dist, MASKED)
        for h, slope in enumerate(slopes):
            bias_ref[cfg, variant, h * Q_BLOCK:(h + 1) * Q_BLOCK] = base * (slope * LOG2E)


def _edge_variant(qb, n_blocks, i, n_tiles):
    variant = 0
    if qb == 0:
        variant = variant + (i == 0).astype(jnp.int32)
    if qb == n_blocks - 1:
        variant = variant + 2 * (i == n_tiles - 1).astype(jnp.int32)
    return variant


def _window(refs, q0, rows, cols, radius):
    prev_ref, cur_ref, next_ref = refs
    lo, hi = q0 - radius, q0 + Q_BLOCK + radius
    parts = []
    if lo < 0:
        parts.append(prev_ref[:, cols])
    parts.append(cur_ref[max(lo, 0):min(hi, rows), cols])
    if hi > rows:
        parts.append(next_ref[:, cols])
    return parts[0] if len(parts) == 1 else jnp.concatenate(parts, axis=0)


def _scores(qblks, kwin, bias_ref, cfg, variant):
    lane = lax.broadcasted_iota(jnp.int32, (1, PAIR), 1)
    head0 = lane < HEAD_DIM
    keep = (head0.astype(BF16), (~head0).astype(BF16))
    q = jnp.concatenate([qblk * k for qblk in qblks for k in keep], axis=0)
    s = lax.dot_general(q, kwin, (((1,), (1,)), ((), ())), preferred_element_type=F32)
    return s - bias_ref[cfg, variant]


def _attend(s, n_pairs, vwin, sinks=None):
    lane = lax.broadcasted_iota(jnp.int32, (1, PAIR), 1)
    head0 = lane < HEAD_DIM
    m = jnp.max(s, axis=-1, keepdims=True)
    if sinks is not None:
        m = jnp.maximum(m, jnp.concatenate([jnp.full((Q_BLOCK, 1), sk, F32) for sk in sinks], axis=0))
    p = jnp.exp2(s - m).astype(BF16)
    rhs = jnp.concatenate([vwin, jnp.ones_like(vwin)], axis=1)
    o = jnp.dot(p, rhs, preferred_element_type=F32)
    m_b = jnp.broadcast_to(m, (s.shape[0], PAIR))
    out = []
    for k in range(n_pairs):
        top = slice(2 * k * Q_BLOCK, (2 * k + 1) * Q_BLOCK)
        bot = slice((2 * k + 1) * Q_BLOCK, (2 * k + 2) * Q_BLOCK)
        pick = lambda a, top=top, bot=bot: jnp.where(head0, a[top], a[bot])
        out.append((pick(o[:, :PAIR]), pick(o[:, PAIR:]), pick(m_b)))
    return out


def _mixer_a_kernel(slopes_ref, *refs, n_tiles):
    nd = len(DILATIONS)
    q_refs = refs[0:nd]
    k_refs = refs[nd:4 * nd]
    v_refs = refs[4 * nd:7 * nd]
    o_ref = refs[7 * nd]
    bias_ref, acc_ref, den_ref, max_ref = refs[7 * nd + 1:]

    hp = pl.program_id(0)
    b = pl.program_id(1)
    i = pl.program_id(2)
    t = TOKEN_TILE_ATTN
    slab = t // MERGE_STRIDE

    @pl.when((b == 0) & (i == 0))
    def _():
        for c, d in enumerate(DILATIONS):
            _build_bias(bias_ref, c, (slopes_ref[2 * hp], slopes_ref[2 * hp + 1]), RADIUS_A, float(d))

    for c, d in reversed(list(enumerate(DILATIONS))):
        rows_d = t // d
        n_blocks = rows_d // Q_BLOCK
        for qb in range(n_blocks):
            q0 = qb * Q_BLOCK
            variant = _edge_variant(qb, n_blocks, i, n_tiles)
            for r in range(d):
                cols = slice(r * PAIR, (r + 1) * PAIR)
                qblk = q_refs[c][q0:q0 + Q_BLOCK, cols]
                kwin = _window(k_refs[3 * c:3 * c + 3], q0, rows_d, cols, RADIUS_A)
                vwin = _window(v_refs[3 * c:3 * c + 3], q0, rows_d, cols, RADIUS_A)
                (acc, den, mx), = _attend(_scores([qblk], kwin, bias_ref, c, variant), 1, vwin)
                if d == 1:
                    rows = pl.ds(q0, Q_BLOCK)
                elif d == MERGE_STRIDE:
                    rows = pl.ds(r * slab + q0, Q_BLOCK)
                else:
                    sub = d // MERGE_STRIDE
                    rows = pl.ds((r % MERGE_STRIDE) * slab + r // MERGE_STRIDE + sub * q0, Q_BLOCK,
                                 stride=sub)
                acc_ref[c, rows, :] = acc
                den_ref[c, rows, :] = den
                max_ref[c, rows, :] = mx

    chunk = 256
    for i0 in range(0, slab, chunk):
        for r in range(MERGE_STRIDE):
            token_rows = pl.ds(r + MERGE_STRIDE * i0, chunk, stride=MERGE_STRIDE)
            slab_rows = pl.ds(r * slab + i0, chunk)
            rows = [token_rows if d == 1 else slab_rows for d in DILATIONS]
            ms = [max_ref[c, rows[c], :] for c in range(nd)]
            top = functools.reduce(jnp.maximum, ms)
            num = None
            den = None
            for c in range(nd):
                w = jnp.exp2(ms[c] - top)
                n_c = w * acc_ref[c, rows[c], :]
                d_c = w * den_ref[c, rows[c], :]
                num = n_c if num is None else num + n_c
                den = d_c if den is None else den + d_c
            o_ref[token_rows, :] = num / den


def _halo_specs(d, rows, total_rows, radius, lead_map):
    per = rows // radius
    last = total_rows // radius - 1
    width = d * PAIR

    def prev(*g):
        return (*lead_map(*g), jnp.maximum(g[-1] * per - 1, 0), 0)

    def cur(*g):
        return (*lead_map(*g), g[-1], 0)

    def nxt(*g):
        return (*lead_map(*g), jnp.minimum((g[-1] + 1) * per, last), 0)

    return [pl.BlockSpec((None, None, radius, width), prev),
            pl.BlockSpec((None, None, rows, width), cur),
            pl.BlockSpec((None, None, radius, width), nxt)]


def _mixer_a(qa, ka, va, slopes, batch, seq):
    t = TOKEN_TILE_ATTN
    n_tiles = seq // t
    lead = lambda hp, b, i: (hp, b)
    views = lambda a: [v.reshape(N_PAIRS, batch, seq // d, d * PAIR) for v, d in zip(a, DILATIONS)]
    in_specs = [pl.BlockSpec(memory_space=pltpu.SMEM)]
    for d in DILATIONS:
        in_specs.append(pl.BlockSpec((None, None, t // d, d * PAIR), lambda hp, b, i: (hp, b, i, 0)))
    halo = []
    for d in DILATIONS:
        halo += _halo_specs(d, t // d, seq // d, RADIUS_A, lead)
    in_specs += halo + halo
    k_views = [v for kv in views(ka) for v in (kv, kv, kv)]
    v_views = [v for vv in views(va) for v in (vv, vv, vv)]
    scratch = [pltpu.VMEM((len(DILATIONS), 4, 2 * Q_BLOCK, Q_BLOCK + 2 * RADIUS_A), F32)]
    scratch += [pltpu.VMEM((len(DILATIONS), t, PAIR), F32)] * 3
    return pl.pallas_call(
        functools.partial(_mixer_a_kernel, n_tiles=n_tiles),
        grid=(N_PAIRS, batch, n_tiles),
        in_specs=in_specs,
        out_specs=pl.BlockSpec((None, t, PAIR), lambda hp, b, i: (b, i, hp)),
        out_shape=jax.ShapeDtypeStruct((batch, seq, WIDTH_A), F32),
        scratch_shapes=scratch,
        compiler_params=pltpu.CompilerParams(
            dimension_semantics=("arbitrary", "arbitrary", "arbitrary"), vmem_limit_bytes=VMEM_LIMIT),
        name="mixer_a",
    )(slopes, *views(qa), *k_views, *v_views)


def _mixer_b_kernel(slopes_ref, sink_ref, q0_ref, q1_ref, kp_ref, kc_ref, kn_ref, vp_ref, vc_ref, vn_ref,
                    o_ref, bias_ref, s_ref, *, n_tiles):
    g = pl.program_id(0)
    b = pl.program_id(1)
    i = pl.program_id(2)
    t = TOKEN_TILE_ATTN
    n_blocks = t // Q_BLOCK
    heads = [4 * g + h for h in range(4)]

    @pl.when((b == 0) & (i == 0))
    def _():
        _build_bias(bias_ref, 0, [slopes_ref[h] for h in heads], RADIUS_B, 1.0)

    sinks = [sink_ref[h] * LOG2E for h in heads]
    lane = lax.broadcasted_iota(jnp.int32, (1, PAIR), 1)
    every = slice(None)

    def scores(qb):
        q0 = qb * Q_BLOCK
        variant = _edge_variant(qb, n_blocks, i, n_tiles)
        qblks = [q_ref[q0:q0 + Q_BLOCK, :] for q_ref in (q0_ref, q1_ref)]
        kwin = _window((kp_ref, kc_ref, kn_ref), q0, t, every, RADIUS_B)
        s_ref[qb % (SCORE_AHEAD + 1)] = _scores(qblks, kwin, bias_ref, 0, variant)

    for qb in range(SCORE_AHEAD):
        scores(qb)
    for qb in range(n_blocks):
        q0 = qb * Q_BLOCK
        if qb + SCORE_AHEAD < n_blocks:
            scores(qb + SCORE_AHEAD)
        vwin = _window((vp_ref, vc_ref, vn_ref), q0, t, every, RADIUS_B)
        stats = _attend(s_ref[qb % (SCORE_AHEAD + 1)], 2, vwin, sinks=sinks)
        for k, (acc, den, mx) in enumerate(stats):
            sink_pair = jnp.where(lane < HEAD_DIM, sinks[2 * k], sinks[2 * k + 1])
            den = den + jnp.exp2(sink_pair - mx)
            o_ref[q0:q0 + Q_BLOCK, k * PAIR:(k + 1) * PAIR] = acc / den


def _mixer_b(qb, kb, vb, slopes, sinks, batch, seq):
    t = TOKEN_TILE_ATTN
    n_tiles = seq // t
    kv_lead = lambda g, b, i: (g, b)
    view = lambda a: a.reshape(a.shape[0], batch, seq, PAIR)
    halo = _halo_specs(1, t, seq, RADIUS_B, kv_lead)
    smem = pl.BlockSpec(memory_space=pltpu.SMEM)
    q_spec = lambda k: pl.BlockSpec((None, None, t, PAIR), lambda g, b, i: (2 * g + k, b, i, 0))
    return pl.pallas_call(
        functools.partial(_mixer_b_kernel, n_tiles=n_tiles),
        grid=(2, batch, n_tiles),
        in_specs=[smem, smem, q_spec(0), q_spec(1)] + halo + halo,
        out_specs=pl.BlockSpec((None, t, 2 * PAIR), lambda g, b, i: (b, i, g)),
        out_shape=jax.ShapeDtypeStruct((batch, seq, WIDTH_B), F32),
        scratch_shapes=[pltpu.VMEM((1, 4, 4 * Q_BLOCK, Q_BLOCK + 2 * RADIUS_B), F32),
                        pltpu.VMEM((SCORE_AHEAD + 1, 4 * Q_BLOCK, Q_BLOCK + 2 * RADIUS_B), F32)],
        compiler_params=pltpu.CompilerParams(
            dimension_semantics=("arbitrary", "arbitrary", "arbitrary"), vmem_limit_bytes=VMEM_LIMIT),
        name="mixer_b",
    )(slopes, sinks, view(qb), view(qb), *([view(kb)] * 3), *([view(vb)] * 3))


def _rms(x, g):
    ms = jnp.mean(x * x, axis=-1, keepdims=True)
    return (x * lax.rsqrt(ms + EPS)) * g


def _outffn_kernel(yap_ref, yac_ref, yan_ref, ybp_ref, ybc_ref, ybn_ref, xp_ref, xc_ref, xn_ref,
                   ga_ref, gb_ref, wo_ref, n2_ref, wu_ref, cw_ref, cb_ref,
                   wd_ref, o_ref, y_ref, h2_ref, perm_ref, hid_ref, *, tiles_per_seq):
    i = pl.program_id(0)
    t = TOKEN_TILE_FFN
    halo = FFN_HALO
    first = (i % tiles_per_seq) == 0
    last = (i % tiles_per_seq) == tiles_per_seq - 1

    for r0, ya_ref, yb_ref in ((0, yap_ref, ybp_ref), (halo, yac_ref, ybc_ref), (halo + t, yan_ref, ybn_ref)):
        n = ya_ref.shape[0]
        y_ref[r0:r0 + n, 0:WIDTH_A] = _rms(ya_ref[...], ga_ref[...]).astype(BF16)
        y_ref[r0:r0 + n, WIDTH_A:] = _rms(yb_ref[...], gb_ref[...]).astype(BF16)
    mixed = jnp.dot(y_ref[...], wo_ref[...], preferred_element_type=F32)

    h_prev = jnp.where(first, 0.0, _rms(xp_ref[...] + mixed[0:halo], n2_ref[...]))
    h_next = jnp.where(last, 0.0, _rms(xn_ref[...] + mixed[halo + t:], n2_ref[...]))
    x1 = xc_ref[...] + mixed[halo:halo + t]
    o_ref[...] = x1

    per = t // 8
    h_cur = _rms(x1, n2_ref[...])
    lane_tiles = D_MODEL // 128
    for c in range(lane_tiles):
        for s in range(8):
            perm_ref[c, pl.ds(s, per, stride=8), :] = h_cur[s * per:(s + 1) * per, c * 128:(c + 1) * 128]
        h2_ref[0:t, c * 128:(c + 1) * 128] = perm_ref[c].astype(BF16)
    h2_ref[t:, :] = jnp.concatenate([h_prev[halo - 8:], h_next[:8]], axis=0).astype(BF16)

    sub = lax.broadcasted_iota(jnp.int32, (8, FF_CHUNK), 0)

    def conv(u, w, b):
        body, edge = u[0:t], u[t:]
        lo = jnp.where(sub == 0, pltpu.roll(edge[0:8], 1, axis=0), pltpu.roll(body[t - 8:], 1, axis=0))
        hi = jnp.where(sub == 7, pltpu.roll(edge[8:16], 7, axis=0), pltpu.roll(body[0:8], 7, axis=0))
        below = jnp.concatenate([lo, body[:t - 8]], axis=0)
        above = jnp.concatenate([body[8:], hi], axis=0)
        return b + below * w[0:1] + body * w[1:2] + above * w[2:3]

    for j in range(N_FF_CHUNKS):
        h2 = h2_ref[...]
        branch = []
        for c0 in (j * FF_CHUNK, D_FF + j * FF_CHUNK):
            cols = slice(c0, c0 + FF_CHUNK)
            u = jnp.dot(h2, wu_ref[:, cols], preferred_element_type=F32)
            branch.append(conv(u, cw_ref[:, cols], cb_ref[:, cols]))
        gate, val = branch
        hid_ref[:, j * FF_CHUNK:(j + 1) * FF_CHUNK] = (jax.nn.silu(gate) * val).astype(BF16)

    down = jnp.dot(hid_ref[...], wd_ref[...], preferred_element_type=F32)
    for c in range(lane_tiles):
        perm_ref[c] = down[:, c * 128:(c + 1) * 128]
        for s in range(8):
            o_ref[s * per:(s + 1) * per, c * 128:(c + 1) * 128] += perm_ref[c, pl.ds(s, per, stride=8), :]


def _outffn(ya, yb, x2, ga, gb, w_out, n2, wu, cw, cb, wd, seq):
    n = x2.shape[0]
    t = TOKEN_TILE_FFN
    per = t // FFN_HALO
    last = n // FFN_HALO - 1
    prev = lambda i: (jnp.maximum(i * per - 1, 0), 0)
    cur = lambda i: (i, 0)
    nxt = lambda i: (jnp.minimum((i + 1) * per, last), 0)

    def halo_specs(width):
        return [pl.BlockSpec((FFN_HALO, width), prev), pl.BlockSpec((t, width), cur),
                pl.BlockSpec((FFN_HALO, width), nxt)]

    def resident(shape):
        return pl.BlockSpec(shape, lambda i: (0,) * len(shape), pipeline_mode=pl.Buffered(1))

    return pl.pallas_call(
        functools.partial(_outffn_kernel, tiles_per_seq=seq // t),
        grid=(n // t,),
        in_specs=halo_specs(WIDTH_A) + halo_specs(WIDTH_B) + halo_specs(D_MODEL) + [
            resident((1, WIDTH_A)), resident((1, WIDTH_B)), resident((D_MODEL, D_MODEL)),
            resident((1, D_MODEL)),
            resident((D_MODEL, 2 * D_FF)), resident((3, 2 * D_FF)), resident((1, 2 * D_FF)),
            resident((D_FF, D_MODEL)),
        ],
        out_specs=pl.BlockSpec((t, D_MODEL), cur),
        out_shape=jax.ShapeDtypeStruct((n, D_MODEL), F32),
        scratch_shapes=[pltpu.VMEM((t + 2 * FFN_HALO, D_MODEL), BF16),
                        pltpu.VMEM((t + FFN_HALO, D_MODEL), BF16),
                        pltpu.VMEM((D_MODEL // 128, t, 128), F32),
                        pltpu.VMEM((t, D_FF), BF16)],
        compiler_params=pltpu.CompilerParams(
            dimension_semantics=("arbitrary",), vmem_limit_bytes=VMEM_LIMIT),
        name="outffn",
    )(ya, ya, ya, yb, yb, yb, x2, x2, x2, ga, gb, w_out, n2, wu, cw, cb, wd)


def _prepare(norm1, w_in, q_norm_a, k_norm_a, q_norm_b, k_norm_b, sink_b, out_norm_a, out_norm_b, w_out,
             norm2, w_up, conv_w, conv_b, w_down):
    scale = HEAD_DIM ** -0.5 * LOG2E
    lanes = np.arange(256)
    seg = (lanes[:, None] // HEAD_DIM == lanes[None, :] // HEAD_DIM).astype(np.float32) / HEAD_DIM
    src = np.arange(PAIR)
    rep = ((src[:, None] // HEAD_DIM == lanes[None, :] // PAIR)
           & (src[:, None] % HEAD_DIM == lanes[None, :] % HEAD_DIM)).astype(np.float32)
    slopes_a, slopes_b = _alibi_slopes()

    return dict(
        n1=norm1.reshape(1, D_MODEL),
        w_in=w_in.astype(BF16),
        seg=jnp.asarray(seg, BF16),
        rep=jnp.asarray(rep, BF16),
        gqa=(jnp.tile(q_norm_a, 4) * scale).reshape(1, 256),
        gka=jnp.tile(k_norm_a, 4).reshape(1, 256),
        gqb=(jnp.tile(q_norm_b, 4) * scale).reshape(1, 256),
        gkb=jnp.tile(k_norm_b, 2).reshape(1, PAIR),
        slopes_a=jnp.asarray(slopes_a),
        slopes_b=jnp.asarray(slopes_b),
        sinks=sink_b.astype(F32),
        ga=out_norm_a.reshape(1, WIDTH_A),
        gb=out_norm_b.reshape(1, WIDTH_B),
        w_out=w_out.astype(BF16),
        n2=norm2.reshape(1, D_MODEL),
        wu=w_up.astype(BF16),
        cw=conv_w,
        cb=conv_b.reshape(1, 2 * D_FF),
        wd=w_down.astype(BF16),
    )


def _layer(x, p):
    batch, seq, _ = x.shape
    assert seq % TOKEN_TILE_ATTN == 0 and seq % TOKEN_TILE_FFN == 0
    x2 = x.reshape(batch * seq, D_MODEL)
    qa, ka, va, qb, kb, vb = _inproj(x2, p["n1"], p["w_in"], p["seg"], p["rep"],
                                     p["gqa"], p["gka"], p["gqb"], p["gkb"])
    ya = _mixer_a(qa, ka, va, p["slopes_a"], batch, seq).reshape(batch * seq, WIDTH_A)
    yb = _mixer_b(qb, kb, vb, p["slopes_b"], p["sinks"], batch, seq).reshape(batch * seq, WIDTH_B)
    out = _outffn(ya, yb, x2, p["ga"], p["gb"], p["w_out"], p["n2"], p["wu"], p["cw"], p["cb"],
                  p["wd"], seq)
    return out.reshape(batch, seq, D_MODEL)


def kernel(x_prompt, x_sample, norm1, w_in, q_norm_a, k_norm_a, q_norm_b, k_norm_b, sink_b, out_norm_a,
           out_norm_b, w_out, norm2, w_up, conv_w, conv_b, w_down):
    y_prompt, y_sample = x_prompt, x_sample
    for l in range(norm1.shape[0]):
        p = _prepare(norm1[l], w_in[l], q_norm_a[l], k_norm_a[l], q_norm_b[l], k_norm_b[l], sink_b[l],
                     out_norm_a[l], out_norm_b[l], w_out[l], norm2[l], w_up[l], conv_w[l], conv_b[l],
                     w_down[l])
        y_prompt = _layer(y_prompt, p)
        y_sample = _layer(y_sample, p)
    return (y_prompt, y_sample)
```

```python
import functools

import numpy as np
import jax
import jax.numpy as jnp
from jax import lax
from jax.experimental import pallas as pl
from jax.experimental.pallas import tpu as pltpu

F32 = jnp.float32
BF16 = jnp.bfloat16

D_MODEL = 1024
HEAD_DIM = 64
PAIR = 2 * HEAD_DIM
N_PAIRS = 4
WIDTH_A = 512
WIDTH_B = 512
KV_WIDTH_B = 128
IN_WIDTH = 2304
DILATIONS = (1, 4, 16)
MERGE_STRIDE = 4
RADIUS_A = 64
RADIUS_B = 128
D_FF = 2816
EPS = 1e-6
MASKED = 1e30
LOG2E = 1.4426950408889634

Q_BLOCK = 128
SCORE_AHEAD = 2
TOKEN_TILE_IN = 1024
TOKEN_TILE_ATTN = 2048
TOKEN_TILE_FFN = 512
FFN_HALO = 16
FF_CHUNK = 256
N_FF_CHUNKS = D_FF // FF_CHUNK
VMEM_LIMIT = 56 * 1024 * 1024


def _alibi_slopes():
    n = 16
    h = np.arange(1, n + 1, dtype=np.float32)
    s = np.exp2(-8.0 * h / n).astype(np.float32)
    return s[8:], s[:8]


def _inproj_kernel(x_ref, n1_ref, w_ref, seg_ref, rep_ref, gqa_ref, gka_ref, gqb_ref, gkb_ref,
                   qa1_ref, qa4_ref, qa16_ref, ka1_ref, ka4_ref, ka16_ref, va1_ref, va4_ref, va16_ref,
                   qb_ref, kb_ref, vb_ref, stage_ref):
    t = TOKEN_TILE_IN
    x = x_ref[...]
    ms = jnp.mean(x * x, axis=-1, keepdims=True)
    h = ((x * lax.rsqrt(ms + EPS)) * n1_ref[...]).astype(BF16)
    seg = seg_ref[...]

    def proj(c0, n):
        return jnp.dot(h, w_ref[:, c0:c0 + n], preferred_element_type=F32)

    def head_norm(p, g, seg_m):
        msq = jnp.dot((p * p).astype(BF16), seg_m, preferred_element_type=F32)
        return (p * lax.rsqrt(msq + EPS)) * g

    def write_pairs(ref, y, first_pair):
        for j in range(y.shape[1] // PAIR):
            ref[first_pair + j] = y[:, j * PAIR:(j + 1) * PAIR].astype(BF16)

    def write_dilated(refs, y, first_pair, first_slot):
        for j in range(y.shape[1] // PAIR):
            pair = first_pair + j
            yp = y[:, j * PAIR:(j + 1) * PAIR]
            refs[0][pair] = yp.astype(BF16)
            stage = stage_ref.at[first_slot + j]
            stage[...] = yp
            for ref, d in zip(refs[1:], DILATIONS[1:]):
                for r in range(d):
                    ref[pair, :, r * PAIR:(r + 1) * PAIR] = stage[pl.ds(r, t // d, stride=d), :].astype(BF16)

    qa_refs = (qa1_ref, qa4_ref, qa16_ref)
    ka_refs = (ka1_ref, ka4_ref, ka16_ref)
    va_refs = (va1_ref, va4_ref, va16_ref)

    def head_norm_wide(p, g):
        halves = [head_norm(p[:, c:c + 256], g, seg) for c in (0, 256)]
        return jnp.concatenate(halves, axis=1)

    write_dilated(qa_refs, head_norm_wide(proj(0, 512), gqa_ref[...]), 0, 0)
    write_dilated(ka_refs, head_norm_wide(proj(512, 512), gka_ref[...]), 0, 4)
    write_dilated(va_refs, proj(1024, 512), 0, 8)
    write_pairs(qb_ref, head_norm_wide(proj(1536, 512), gqb_ref[...]), 0)

    rep = rep_ref[...]
    kvb = proj(2048, 2 * KV_WIDTH_B)
    kb = head_norm(kvb[:, :KV_WIDTH_B], gkb_ref[...], seg[:PAIR, :PAIR]).astype(BF16)
    vb = kvb[:, KV_WIDTH_B:].astype(BF16)
    write_pairs(kb_ref, jnp.dot(kb, rep, preferred_element_type=F32), 0)
    write_pairs(vb_ref, jnp.dot(vb, rep, preferred_element_type=F32), 0)


def _inproj(x2, n1, w_in, seg, rep, gqa, gka, gqb, gkb):
    n = x2.shape[0]
    t = TOKEN_TILE_IN
    const = lambda i: (0, 0)
    pair_out = lambda p, d=1: pl.BlockSpec((p, t // d, d * PAIR), lambda i: (0, i, 0))
    pair_shape = lambda p, d=1: jax.ShapeDtypeStruct((p, n // d, d * PAIR), BF16)
    dilated = [(N_PAIRS, d) for _ in range(3) for d in DILATIONS]
    outs = dilated + [(N_PAIRS, 1), (2, 1), (2, 1)]
    res = pl.pallas_call(
        _inproj_kernel,
        grid=(n // t,),
        in_specs=[
            pl.BlockSpec((t, D_MODEL), lambda i: (i, 0)),
            pl.BlockSpec((1, D_MODEL), const),
            pl.BlockSpec((D_MODEL, IN_WIDTH), const, pipeline_mode=pl.Buffered(1)),
            pl.BlockSpec((256, 256), const),
            pl.BlockSpec((PAIR, 256), const),
            pl.BlockSpec((1, 256), const),
            pl.BlockSpec((1, 256), const),
            pl.BlockSpec((1, 256), const),
            pl.BlockSpec((1, PAIR), const),
        ],
        out_specs=[pair_out(p, d) for p, d in outs],
        out_shape=[pair_shape(p, d) for p, d in outs],
        scratch_shapes=[pltpu.VMEM((3 * N_PAIRS, t, PAIR), F32)],
        compiler_params=pltpu.CompilerParams(
            dimension_semantics=("arbitrary",), vmem_limit_bytes=VMEM_LIMIT),
        name="inproj",
    )(x2, n1, w_in, seg, rep, gqa, gka, gqb, gkb)
    return res[0:3], res[3:6], res[6:9], res[9], res[10], res[11]


def _build_bias(bias_ref, cfg, slopes, radius, dist_scale):
    width = Q_BLOCK + 2 * radius
    row = lax.broadcasted_iota(jnp.int32, (Q_BLOCK, width), 0)
    col = lax.broadcasted_iota(jnp.int32, (Q_BLOCK, width), 1)
    rel = col - radius - row
    band = jnp.abs(rel) <= radius
    dist = jnp.abs(rel).astype(F32) * dist_scale
    for variant in range(4):
        ok = band
        if variant & 1:
            ok = ok & (col >= radius)
        if variant & 2:
            ok = ok & (col < Q_BLOCK + radius)
        base = jnp.where(ok, dist, MASKED)
        for h, slope in enumerate(slopes):
            bias_ref[cfg, variant, h * Q_BLOCK:(h + 1) * Q_BLOCK] = base * (slope * LOG2E)


def _edge_variant(qb, n_blocks, i, n_tiles):
    variant = 0
    if qb == 0:
        variant = variant + (i == 0).astype(jnp.int32)
    if qb == n_blocks - 1:
        variant = variant + 2 * (i == n_tiles - 1).astype(jnp.int32)
    return variant


def _window(refs, q0, rows, cols, radius):
    prev_ref, cur_ref, next_ref = refs
    lo, hi = q0 - radius, q0 + Q_BLOCK + radius
    parts = []
    if lo < 0:
        parts.append(prev_ref[:, cols])
    parts.append(cur_ref[max(lo, 0):min(hi, rows), cols])
    if hi > rows:
        parts.append(next_ref[:, cols])
    return parts[0] if len(parts) == 1 else jnp.concatenate(parts, axis=0)


def _scores(qblks, kwin, bias_ref, cfg, variant):
    lane = lax.broadcasted_iota(jnp.int32, (1, PAIR), 1)
    head0 = lane < HEAD_DIM
    keep = (head0.astype(BF16), (~head0).astype(BF16))
    q = jnp.concatenate([qblk * k for qblk in qblks for k in keep], axis=0)
    s = lax.dot_general(q, kwin, (((1,), (1,)), ((), ())), preferred_element_type=F32)
    return s - bias_ref[cfg, variant]


def _attend(s, n_pairs, vwin, sinks=None):
    lane = lax.broadcasted_iota(jnp.int32, (1, PAIR), 1)
    head0 = lane < HEAD_DIM
    m = jnp.max(s, axis=-1, keepdims=True)
    if sinks is not None:
        m = jnp.maximum(m, jnp.concatenate([jnp.full((Q_BLOCK, 1), sk, F32) for sk in sinks], axis=0))
    p = jnp.exp2(s - m).astype(BF16)
    rhs = jnp.concatenate([vwin, jnp.ones_like(vwin)], axis=1)
    o = jnp.dot(p, rhs, preferred_element_type=F32)
    m_b = jnp.broadcast_to(m, (s.shape[0], PAIR))
    out = []
    for k in range(n_pairs):
        top = slice(2 * k * Q_BLOCK, (2 * k + 1) * Q_BLOCK)
        bot = slice((2 * k + 1) * Q_BLOCK, (2 * k + 2) * Q_BLOCK)
        pick = lambda a, top=top, bot=bot: jnp.where(head0, a[top], a[bot])
        out.append((pick(o[:, :PAIR]), pick(o[:, PAIR:]), pick(m_b)))
    return out


def _mixer_a_kernel(slopes_ref, *refs, n_tiles):
    nd = len(DILATIONS)
    q_refs = refs[0:nd]
    k_refs = refs[nd:4 * nd]
    v_refs = refs[4 * nd:7 * nd]
    o_ref = refs[7 * nd]
    bias_ref, acc_ref, den_ref, max_ref = refs[7 * nd + 1:]

    hp = pl.program_id(0)
    b = pl.program_id(1)
    i = pl.program_id(2)
    t = TOKEN_TILE_ATTN
    slab = t // MERGE_STRIDE

    @pl.when((b == 0) & (i == 0))
    def _():
        for c, d in enumerate(DILATIONS):
            _build_bias(bias_ref, c, (slopes_ref[2 * hp], slopes_ref[2 * hp + 1]), RADIUS_A, float(d))

    for c, d in reversed(list(enumerate(DILATIONS))):
        rows_d = t // d
        n_blocks = rows_d // Q_BLOCK
        for qb in range(n_blocks):
            q0 = qb * Q_BLOCK
            variant = _edge_variant(qb, n_blocks, i, n_tiles)
            for r in range(d):
                cols = slice(r * PAIR, (r + 1) * PAIR)
                qblk = q_refs[c][q0:q0 + Q_BLOCK, cols]
                kwin = _window(k_refs[3 * c:3 * c + 3], q0, rows_d, cols, RADIUS_A)
                vwin = _window(v_refs[3 * c:3 * c + 3], q0, rows_d, cols, RADIUS_A)
                (acc, den, mx), = _attend(_scores([qblk], kwin, bias_ref, c, variant), 1, vwin)
                if d == 1:
                    rows = pl.ds(q0, Q_BLOCK)
                elif d == MERGE_STRIDE:
                    rows = pl.ds(r * slab + q0, Q_BLOCK)
                else:
                    sub = d // MERGE_STRIDE
                    rows = pl.ds((r % MERGE_STRIDE) * slab + r // MERGE_STRIDE + sub * q0, Q_BLOCK,
                                 stride=sub)
                acc_ref[c, rows, :] = acc
                den_ref[c, rows, :] = den
                max_ref[c, rows, :] = mx

    chunk = 256
    for i0 in range(0, slab, chunk):
        for r in range(MERGE_STRIDE):
            token_rows = pl.ds(r + MERGE_STRIDE * i0, chunk, stride=MERGE_STRIDE)
            slab_rows = pl.ds(r * slab + i0, chunk)
            rows = [token_rows if d == 1 else slab_rows for d in DILATIONS]
            ms = [max_ref[c, rows[c], :] for c in range(nd)]
            top = functools.reduce(jnp.maximum, ms)
            num = None
            den = None
            for c in range(nd):
                w = jnp.exp2(ms[c] - top)
                n_c = w * acc_ref[c, rows[c], :]
                d_c = w * den_ref[c, rows[c], :]
                num = n_c if num is None else num + n_c
                den = d_c if den is None else den + d_c
            o_ref[token_rows, :] = num / den


def _halo_specs(d, rows, total_rows, radius, lead_map):
    per = rows // radius
    last = total_rows // radius - 1
    width = d * PAIR

    def prev(*g):
        return (*lead_map(*g), jnp.maximum(g[-1] * per - 1, 0), 0)

    def cur(*g):
        return (*lead_map(*g), g[-1], 0)

    def nxt(*g):
        return (*lead_map(*g), jnp.minimum((g[-1] + 1) * per, last), 0)

    return [pl.BlockSpec((None, None, radius, width), prev),
            pl.BlockSpec((None, None, rows, width), cur),
            pl.BlockSpec((None, None, radius, width), nxt)]


def _mixer_a(qa, ka, va, slopes, batch, seq):
    t = TOKEN_TILE_ATTN
    n_tiles = seq // t
    lead = lambda hp, b, i: (hp, b)
    views = lambda a: [v.reshape(N_PAIRS, batch, seq // d, d * PAIR) for v, d in zip(a, DILATIONS)]
    in_specs = [pl.BlockSpec(memory_space=pltpu.SMEM)]
    for d in DILATIONS:
        in_specs.append(pl.BlockSpec((None, None, t // d, d * PAIR), lambda hp, b, i: (hp, b, i, 0)))
    halo = []
    for d in DILATIONS:
        halo += _halo_specs(d, t // d, seq // d, RADIUS_A, lead)
    in_specs += halo + halo
    k_views = [v for kv in views(ka) for v in (kv, kv, kv)]
    v_views = [v for vv in views(va) for v in (vv, vv, vv)]
    scratch = [pltpu.VMEM((len(DILATIONS), 4, 2 * Q_BLOCK, Q_BLOCK + 2 * RADIUS_A), F32)]
    scratch += [pltpu.VMEM((len(DILATIONS), t, PAIR), F32)] * 3
    return pl.pallas_call(
        functools.partial(_mixer_a_kernel, n_tiles=n_tiles),
        grid=(N_PAIRS, batch, n_tiles),
        in_specs=in_specs,
        out_specs=pl.BlockSpec((None, t, PAIR), lambda hp, b, i: (b, i, hp)),
        out_shape=jax.ShapeDtypeStruct((batch, seq, WIDTH_A), F32),
        scratch_shapes=scratch,
        compiler_params=pltpu.CompilerParams(
            dimension_semantics=("arbitrary", "arbitrary", "arbitrary"), vmem_limit_bytes=VMEM_LIMIT),
        name="mixer_a",
    )(slopes, *views(qa), *k_views, *v_views)


def _mixer_b_kernel(slopes_ref, sink_ref, q0_ref, q1_ref, kp_ref, kc_ref, kn_ref, vp_ref, vc_ref, vn_ref,
                    o_ref, bias_ref, s_ref, *, n_tiles):
    g = pl.program_id(0)
    b = pl.program_id(1)
    i = pl.program_id(2)
    t = TOKEN_TILE_ATTN
    n_blocks = t // Q_BLOCK
    heads = [4 * g + h for h in range(4)]

    @pl.when((b == 0) & (i == 0))
    def _():
        _build_bias(bias_ref, 0, [slopes_ref[h] for h in heads], RADIUS_B, 1.0)

    sinks = [sink_ref[h] * LOG2E for h in heads]
    lane = lax.broadcasted_iota(jnp.int32, (1, PAIR), 1)
    every = slice(None)

    def scores(qb):
        q0 = qb * Q_BLOCK
        variant = _edge_variant(qb, n_blocks, i, n_tiles)
        qblks = [q_ref[q0:q0 + Q_BLOCK, :] for q_ref in (q0_ref, q1_ref)]
        kwin = _window((kp_ref, kc_ref, kn_ref), q0, t, every, RADIUS_B)
        s_ref[qb % (SCORE_AHEAD + 1)] = _scores(qblks, kwin, bias_ref, 0, variant)

    for qb in range(SCORE_AHEAD):
        scores(qb)
    for qb in range(n_blocks):
        q0 = qb * Q_BLOCK
        if qb + SCORE_AHEAD < n_blocks:
            scores(qb + SCORE_AHEAD)
        vwin = _window((vp_ref, vc_ref, vn_ref), q0, t, every, RADIUS_B)
        stats = _attend(s_ref[qb % (SCORE_AHEAD + 1)], 2, vwin, sinks=sinks)
        for k, (acc, den, mx) in enumerate(stats):
            sink_pair = jnp.where(lane < HEAD_DIM, sinks[2 * k], sinks[2 * k + 1])
            den = den + jnp.exp2(sink_pair - mx)
            o_ref[q0:q0 + Q_BLOCK, k * PAIR:(k + 1) * PAIR] = acc / den


def _mixer_b(qb, kb, vb, slopes, sinks, batch, seq):
    t = TOKEN_TILE_ATTN
    n_tiles = seq // t
    kv_lead = lambda g, b, i: (g, b)
    view = lambda a: a.reshape(a.shape[0], batch, seq, PAIR)
    halo = _halo_specs(1, t, seq, RADIUS_B, kv_lead)
    smem = pl.BlockSpec(memory_space=pltpu.SMEM)
    q_spec = lambda k: pl.BlockSpec((None, None, t, PAIR), lambda g, b, i: (2 * g + k, b, i, 0))
    return pl.pallas_call(
        functools.partial(_mixer_b_kernel, n_tiles=n_tiles),
        grid=(2, batch, n_tiles),
        in_specs=[smem, smem, q_spec(0), q_spec(1)] + halo + halo,
        out_specs=pl.BlockSpec((None, t, 2 * PAIR), lambda g, b, i: (b, i, g)),
        out_shape=jax.ShapeDtypeStruct((batch, seq, WIDTH_B), F32),
        scratch_shapes=[pltpu.VMEM((1, 4, 4 * Q_BLOCK, Q_BLOCK + 2 * RADIUS_B), F32),
                        pltpu.VMEM((SCORE_AHEAD + 1, 4 * Q_BLOCK, Q_BLOCK + 2 * RADIUS_B), F32)],
        compiler_params=pltpu.CompilerParams(
            dimension_semantics=("arbitrary", "arbitrary", "arbitrary"), vmem_limit_bytes=VMEM_LIMIT),
        name="mixer_b",
    )(slopes, sinks, view(qb), view(qb), *([view(kb)] * 3), *([view(vb)] * 3))


def _rms(x, g):
    ms = jnp.mean(x * x, axis=-1, keepdims=True)
    return (x * lax.rsqrt(ms + EPS)) * g


def _outffn_kernel(yap_ref, yac_ref, yan_ref, ybp_ref, ybc_ref, ybn_ref, xp_ref, xc_ref, xn_ref,
                   ga_ref, gb_ref, wo_ref, n2_ref, wu_ref, cw_ref, cb_ref,
                   wd_ref, o_ref, y_ref, h2_ref, perm_ref, hid_ref, *, tiles_per_seq):
    i = pl.program_id(0)
    t = TOKEN_TILE_FFN
    halo = FFN_HALO
    first = (i % tiles_per_seq) == 0
    last = (i % tiles_per_seq) == tiles_per_seq - 1

    for r0, ya_ref, yb_ref in ((0, yap_ref, ybp_ref), (halo, yac_ref, ybc_ref), (halo + t, yan_ref, ybn_ref)):
        n = ya_ref.shape[0]
        y_ref[r0:r0 + n, 0:WIDTH_A] = _rms(ya_ref[...], ga_ref[...]).astype(BF16)
        y_ref[r0:r0 + n, WIDTH_A:] = _rms(yb_ref[...], gb_ref[...]).astype(BF16)
    mixed = jnp.dot(y_ref[...], wo_ref[...], preferred_element_type=F32)

    h_prev = jnp.where(first, 0.0, _rms(xp_ref[...] + mixed[0:halo], n2_ref[...]))
    h_next = jnp.where(last, 0.0, _rms(xn_ref[...] + mixed[halo + t:], n2_ref[...]))
    x1 = xc_ref[...] + mixed[halo:halo + t]
    o_ref[...] = x1

    per = t // 8
    h_cur = _rms(x1, n2_ref[...])
    lane_tiles = D_MODEL // 128
    for c in range(lane_tiles):
        for s in range(8):
            perm_ref[c, pl.ds(s, per, stride=8), :] = h_cur[s * per:(s + 1) * per, c * 128:(c + 1) * 128]
        h2_ref[0:t, c * 128:(c + 1) * 128] = perm_ref[c].astype(BF16)
    h2_ref[t:, :] = jnp.concatenate([h_prev[halo - 8:], h_next[:8]], axis=0).astype(BF16)

    sub = lax.broadcasted_iota(jnp.int32, (8, FF_CHUNK), 0)

    def conv(u, w, b):
        body, edge = u[0:t], u[t:]
        lo = jnp.where(sub == 0, pltpu.roll(edge[0:8], 1, axis=0), pltpu.roll(body[t - 8:], 1, axis=0))
        hi = jnp.where(sub == 7, pltpu.roll(edge[8:16], 7, axis=0), pltpu.roll(body[0:8], 7, axis=0))
        below = jnp.concatenate([lo, body[:t - 8]], axis=0)
        above = jnp.concatenate([body[8:], hi], axis=0)
        return b + below * w[0:1] + body * w[1:2] + above * w[2:3]

    for j in range(N_FF_CHUNKS):
        h2 = h2_ref[...]
        branch = []
        for c0 in (j * FF_CHUNK, D_FF + j * FF_CHUNK):
            cols = slice(c0, c0 + FF_CHUNK)
            u = jnp.dot(h2, wu_ref[:, cols], preferred_element_type=F32)
            branch.append(conv(u, cw_ref[:, cols], cb_ref[:, cols]))
        gate, val = branch
        hid_ref[:, j * FF_CHUNK:(j + 1) * FF_CHUNK] = (jax.nn.silu(gate) * val).astype(BF16)

    down = jnp.dot(hid_ref[...], wd_ref[...], preferred_element_type=F32)
    for c in range(lane_tiles):
        perm_ref[c] = down[:, c * 128:(c + 1) * 128]
        for s in range(8):
            o_ref[s * per:(s + 1) * per, c * 128:(c + 1) * 128] += perm_ref[c, pl.ds(s, per, stride=8), :]


def _outffn(ya, yb, x2, ga, gb, w_out, n2, wu, cw, cb, wd, seq):
    n = x2.shape[0]
    t = TOKEN_TILE_FFN
    per = t // FFN_HALO
    last = n // FFN_HALO - 1
    prev = lambda i: (jnp.maximum(i * per - 1, 0), 0)
    cur = lambda i: (i, 0)
    nxt = lambda i: (jnp.minimum((i + 1) * per, last), 0)

    def halo_specs(width):
        return [pl.BlockSpec((FFN_HALO, width), prev), pl.BlockSpec((t, width), cur),
                pl.BlockSpec((FFN_HALO, width), nxt)]

    def resident(shape):
        return pl.BlockSpec(shape, lambda i: (0,) * len(shape), pipeline_mode=pl.Buffered(1))

    return pl.pallas_call(
        functools.partial(_outffn_kernel, tiles_per_seq=seq // t),
        grid=(n // t,),
        in_specs=halo_specs(WIDTH_A) + halo_specs(WIDTH_B) + halo_specs(D_MODEL) + [
            resident((1, WIDTH_A)), resident((1, WIDTH_B)), resident((D_MODEL, D_MODEL)),
            resident((1, D_MODEL)),
            resident((D_MODEL, 2 * D_FF)), resident((3, 2 * D_FF)), resident((1, 2 * D_FF)),
            resident((D_FF, D_MODEL)),
        ],
        out_specs=pl.BlockSpec((t, D_MODEL), cur),
        out_shape=jax.ShapeDtypeStruct((n, D_MODEL), F32),
        scratch_shapes=[pltpu.VMEM((t + 2 * FFN_HALO, D_MODEL), BF16),
                        pltpu.VMEM((t + FFN_HALO, D_MODEL), BF16),
                        pltpu.VMEM((D_MODEL // 128, t, 128), F32),
                        pltpu.VMEM((t, D_FF), BF16)],
        compiler_params=pltpu.CompilerParams(
            dimension_semantics=("arbitrary",), vmem_limit_bytes=VMEM_LIMIT),
        name="outffn",
    )(ya, ya, ya, yb, yb, yb, x2, x2, x2, ga, gb, w_out, n2, wu, cw, cb, wd)


def _prepare(norm1, w_in, q_norm_a, k_norm_a, q_norm_b, k_norm_b, sink_b, out_norm_a, out_norm_b, w_out,
             norm2, w_up, conv_w, conv_b, w_down):
    scale = HEAD_DIM ** -0.5 * LOG2E
    lanes = np.arange(256)
    seg = (lanes[:, None] // HEAD_DIM == lanes[None, :] // HEAD_DIM).astype(np.float32) / HEAD_DIM
    src = np.arange(PAIR)
    rep = ((src[:, None] // HEAD_DIM == lanes[None, :] // PAIR)
           & (src[:, None] % HEAD_DIM == lanes[None, :] % HEAD_DIM)).astype(np.float32)
    slopes_a, slopes_b = _alibi_slopes()

    return dict(
        n1=norm1.reshape(1, D_MODEL),
        w_in=w_in.astype(BF16),
        seg=jnp.asarray(seg, BF16),
        rep=jnp.asarray(rep, BF16),
        gqa=(jnp.tile(q_norm_a, 4) * scale).reshape(1, 256),
        gka=jnp.tile(k_norm_a, 4).reshape(1, 256),
        gqb=(jnp.tile(q_norm_b, 4) * scale).reshape(1, 256),
        gkb=jnp.tile(k_norm_b, 2).reshape(1, PAIR),
        slopes_a=jnp.asarray(slopes_a),
        slopes_b=jnp.asarray(slopes_b),
        sinks=sink_b.astype(F32),
        ga=out_norm_a.reshape(1, WIDTH_A),
        gb=out_norm_b.reshape(1, WIDTH_B),
        w_out=w_out.astype(BF16),
        n2=norm2.reshape(1, D_MODEL),
        wu=w_up.astype(BF16),
        cw=conv_w,
        cb=conv_b.reshape(1, 2 * D_FF),
        wd=w_down.astype(BF16),
    )


def _layer(x, p):
    batch, seq, _ = x.shape
    assert seq % TOKEN_TILE_ATTN == 0 and seq % TOKEN_TILE_FFN == 0
    x2 = x.reshape(batch * seq, D_MODEL)
    qa, ka, va, qb, kb, vb = _inproj(x2, p["n1"], p["w_in"], p["seg"], p["rep"],
                                     p["gqa"], p["gka"], p["gqb"], p["gkb"])
    ya = _mixer_a(qa, ka, va, p["slopes_a"], batch, seq).reshape(batch * seq, WIDTH_A)
    yb = _mixer_b(qb, kb, vb, p["slopes_b"], p["sinks"], batch, seq).reshape(batch * seq, WIDTH_B)
    out = _outffn(ya, yb, x2, p["ga"], p["gb"], p["w_out"], p["n2"], p["wu"], p["cw"], p["cb"],
                  p["wd"], seq)
    return out.reshape(batch, seq, D_MODEL)


def kernel(x_prompt, x_sample, norm1, w_in, q_norm_a, k_norm_a, q_norm_b, k_norm_b, sink_b, out_norm_a,
           out_norm_b, w_out, norm2, w_up, conv_w, conv_b, w_down):
    y_prompt, y_sample = x_prompt, x_sample
    for l in range(norm1.shape[0]):
        p = _prepare(norm1[l], w_in[l], q_norm_a[l], k_norm_a[l], q_norm_b[l], k_norm_b[l], sink_b[l],
                     out_norm_a[l], out_norm_b[l], w_out[l], norm2[l], w_up[l], conv_w[l], conv_b[l],
                     w_down[l])
        y_prompt = _layer(y_prompt, p)
        y_sample = _layer(y_sample, p)
    return (y_prompt, y_sample)
```

```python
import functools

import numpy as np
import jax
import jax.numpy as jnp
from jax import lax
from jax.experimental import pallas as pl
from jax.experimental.pallas import tpu as pltpu

F32 = jnp.float32
BF16 = jnp.bfloat16

D_MODEL = 1024
HEAD_DIM = 64
PAIR = 2 * HEAD_DIM
N_PAIRS = 4
WIDTH_A = 512
WIDTH_B = 512
KV_WIDTH_B = 128
IN_WIDTH = 2304
DILATIONS = (1, 4, 16)
MERGE_STRIDE = 4
RADIUS_A = 64
RADIUS_B = 128
D_FF = 2816
EPS = 1e-6
MASKED = 1e30
LOG2E = 1.4426950408889634

Q_BLOCK = 128
SCORE_AHEAD = 2
TOKEN_TILE_IN = 1024
TOKEN_TILE_ATTN = 2048
TOKEN_TILE_FFN = 512
FFN_HALO = 16
FF_CHUNK = 256
N_FF_CHUNKS = D_FF // FF_CHUNK
VMEM_LIMIT = 56 * 1024 * 1024


def _alibi_slopes():
    n = 16
    h = np.arange(1, n + 1, dtype=np.float32)
    s = np.exp2(-8.0 * h / n).astype(np.float32)
    return s[8:], s[:8]


def _inproj_kernel(x_ref, n1_ref, w_ref, rep_ref, gqa_ref, gka_ref, gqb_ref, gkb_ref,
                   qa1_ref, qa4_ref, qa16_ref, ka1_ref, ka4_ref, ka16_ref, va1_ref, va4_ref, va16_ref,
                   qb_ref, kb_ref, vb_ref, stage_ref):
    t = TOKEN_TILE_IN
    x = x_ref[...]
    ms = jnp.mean(x * x, axis=-1, keepdims=True)
    h = ((x * lax.rsqrt(ms + EPS)) * n1_ref[...]).astype(BF16)
    lane = lax.broadcasted_iota(jnp.int32, (1, PAIR), 1)
    head0 = lane < HEAD_DIM

    def proj(c0, n):
        return jnp.dot(h, w_ref[:, c0:c0 + n], preferred_element_type=F32)

    def head_norm(p, g):
        slabs = []
        for c in range(0, p.shape[1], PAIR):
            sq = p[:, c:c + PAIR] * p[:, c:c + PAIR]
            lo = jnp.sum(jnp.where(head0, sq, 0.0), axis=-1, keepdims=True)
            hi = jnp.sum(jnp.where(head0, 0.0, sq), axis=-1, keepdims=True)
            slabs.append(jnp.where(head0, lo, hi) * (1.0 / HEAD_DIM))
        msq = slabs[0] if len(slabs) == 1 else jnp.concatenate(slabs, axis=1)
        return (p * lax.rsqrt(msq + EPS)) * g

    def write_pairs(ref, y, first_pair):
        for j in range(y.shape[1] // PAIR):
            ref[first_pair + j] = y[:, j * PAIR:(j + 1) * PAIR].astype(BF16)

    def write_dilated(refs, y, first_pair, first_slot):
        for j in range(y.shape[1] // PAIR):
            pair = first_pair + j
            yp = y[:, j * PAIR:(j + 1) * PAIR]
            refs[0][pair] = yp.astype(BF16)
            stage = stage_ref.at[first_slot + j]
            stage[...] = yp
            for ref, d in zip(refs[1:], DILATIONS[1:]):
                for r in range(d):
                    ref[pair, :, r * PAIR:(r + 1) * PAIR] = stage[pl.ds(r, t // d, stride=d), :].astype(BF16)

    qa_refs = (qa1_ref, qa4_ref, qa16_ref)
    ka_refs = (ka1_ref, ka4_ref, ka16_ref)
    va_refs = (va1_ref, va4_ref, va16_ref)

    def head_norm_wide(p, g):
        halves = [head_norm(p[:, c:c + 256], g) for c in (0, 256)]
        return jnp.concatenate(halves, axis=1)

    write_dilated(qa_refs, head_norm_wide(proj(0, 512), gqa_ref[...]), 0, 0)
    write_dilated(ka_refs, head_norm_wide(proj(512, 512), gka_ref[...]), 0, 4)
    write_dilated(va_refs, proj(1024, 512), 0, 8)
    write_pairs(qb_ref, head_norm_wide(proj(1536, 512), gqb_ref[...]), 0)

    rep = rep_ref[...]
    kvb = proj(2048, 2 * KV_WIDTH_B)
    kb = head_norm(kvb[:, :KV_WIDTH_B], gkb_ref[...]).astype(BF16)
    vb = kvb[:, KV_WIDTH_B:].astype(BF16)
    write_pairs(kb_ref, jnp.dot(kb, rep, preferred_element_type=F32), 0)
    write_pairs(vb_ref, jnp.dot(vb, rep, preferred_element_type=F32), 0)


def _inproj(x2, n1, w_in, rep, gqa, gka, gqb, gkb):
    n = x2.shape[0]
    t = TOKEN_TILE_IN
    const = lambda i: (0, 0)
    pair_out = lambda p, d=1: pl.BlockSpec((p, t // d, d * PAIR), lambda i: (0, i, 0))
    pair_shape = lambda p, d=1: jax.ShapeDtypeStruct((p, n // d, d * PAIR), BF16)
    dilated = [(N_PAIRS, d) for _ in range(3) for d in DILATIONS]
    outs = dilated + [(N_PAIRS, 1), (2, 1), (2, 1)]
    res = pl.pallas_call(
        _inproj_kernel,
        grid=(n // t,),
        in_specs=[
            pl.BlockSpec((t, D_MODEL), lambda i: (i, 0)),
            pl.BlockSpec((1, D_MODEL), const),
            pl.BlockSpec((D_MODEL, IN_WIDTH), const, pipeline_mode=pl.Buffered(1)),
            pl.BlockSpec((PAIR, 256), const),
            pl.BlockSpec((1, 256), const),
            pl.BlockSpec((1, 256), const),
            pl.BlockSpec((1, 256), const),
            pl.BlockSpec((1, PAIR), const),
        ],
        out_specs=[pair_out(p, d) for p, d in outs],
        out_shape=[pair_shape(p, d) for p, d in outs],
        scratch_shapes=[pltpu.VMEM((3 * N_PAIRS, t, PAIR), F32)],
        compiler_params=pltpu.CompilerParams(
            dimension_semantics=("arbitrary",), vmem_limit_bytes=VMEM_LIMIT),
        name="inproj",
    )(x2, n1, w_in, rep, gqa, gka, gqb, gkb)
    return res[0:3], res[3:6], res[6:9], res[9], res[10], res[11]


def _build_bias(bias_ref, cfg, slopes, radius, dist_scale):
    width = Q_BLOCK + 2 * radius
    row = lax.broadcasted_iota(jnp.int32, (Q_BLOCK, width), 0)
    col = lax.broadcasted_iota(jnp.int32, (Q_BLOCK, width), 1)
    rel = col - radius - row
    band = jnp.abs(rel) <= radius
    dist = jnp.abs(rel).astype(F32) * dist_scale
    for variant in range(4):
        ok = band
        if variant & 1:
            ok = ok & (col >= radius)
        if variant & 2:
            ok = ok & (col < Q_BLOCK + radius)
        base = jnp.where(ok, dist, MASKED)
        for h, slope in enumerate(slopes):
            bias_ref[cfg, variant, h * Q_BLOCK:(h + 1) * Q_BLOCK] = base * (slope * LOG2E)


def _edge_variant(qb, n_blocks, i, n_tiles):
    variant = 0
    if qb == 0:
        variant = variant + (i == 0).astype(jnp.int32)
    if qb == n_blocks - 1:
        variant = variant + 2 * (i == n_tiles - 1).astype(jnp.int32)
    return variant


def _window(refs, q0, rows, cols, radius):
    prev_ref, cur_ref, next_ref = refs
    lo, hi = q0 - radius, q0 + Q_BLOCK + radius
    parts = []
    if lo < 0:
        parts.append(prev_ref[:, cols])
    parts.append(cur_ref[max(lo, 0):min(hi, rows), cols])
    if hi > rows:
        parts.append(next_ref[:, cols])
    return parts[0] if len(parts) == 1 else jnp.concatenate(parts, axis=0)


def _scores(qblks, kwin, bias_ref, cfg, variant):
    lane = lax.broadcasted_iota(jnp.int32, (1, PAIR), 1)
    head0 = lane < HEAD_DIM
    keep = (head0.astype(BF16), (~head0).astype(BF16))
    q = jnp.concatenate([qblk * k for qblk in qblks for k in keep], axis=0)
    s = lax.dot_general(q, kwin, (((1,), (1,)), ((), ())), preferred_element_type=F32)
    return s - bias_ref[cfg, variant]


def _attend(s, n_pairs, vwin, sinks=None):
    lane = lax.broadcasted_iota(jnp.int32, (1, PAIR), 1)
    head0 = lane < HEAD_DIM
    m = jnp.max(s, axis=-1, keepdims=True)
    if sinks is not None:
        m = jnp.maximum(m, jnp.concatenate([jnp.full((Q_BLOCK, 1), sk, F32) for sk in sinks], axis=0))
    p = jnp.exp2(s - m).astype(BF16)
    rhs = jnp.concatenate([vwin, jnp.ones_like(vwin)], axis=1)
    o = jnp.dot(p, rhs, preferred_element_type=F32)
    m_b = jnp.broadcast_to(m, (s.shape[0], PAIR))
    out = []
    for k in range(n_pairs):
        top = slice(2 * k * Q_BLOCK, (2 * k + 1) * Q_BLOCK)
        bot = slice((2 * k + 1) * Q_BLOCK, (2 * k + 2) * Q_BLOCK)
        pick = lambda a, top=top, bot=bot: jnp.where(head0, a[top], a[bot])
        out.append((pick(o[:, :PAIR]), pick(o[:, PAIR:]), pick(m_b)))
    return out


def _mixer_a_kernel(slopes_ref, *refs, n_tiles):
    nd = len(DILATIONS)
    q_refs = refs[0:nd]
    k_refs = refs[nd:4 * nd]
    v_refs = refs[4 * nd:7 * nd]
    o_ref = refs[7 * nd]
    bias_ref, acc_ref, den_ref, max_ref = refs[7 * nd + 1:]

    hp = pl.program_id(0)
    b = pl.program_id(1)
    i = pl.program_id(2)
    t = TOKEN_TILE_ATTN
    slab = t // MERGE_STRIDE

    @pl.when((b == 0) & (i == 0))
    def _():
        for c, d in enumerate(DILATIONS):
            _build_bias(bias_ref, c, (slopes_ref[2 * hp], slopes_ref[2 * hp + 1]), RADIUS_A, float(d))

    for c, d in reversed(list(enumerate(DILATIONS))):
        rows_d = t // d
        n_blocks = rows_d // Q_BLOCK
        for qb in range(n_blocks):
            q0 = qb * Q_BLOCK
            variant = _edge_variant(qb, n_blocks, i, n_tiles)
            for r in range(d):
                cols = slice(r * PAIR, (r + 1) * PAIR)
                qblk = q_refs[c][q0:q0 + Q_BLOCK, cols]
                kwin = _window(k_refs[3 * c:3 * c + 3], q0, rows_d, cols, RADIUS_A)
                vwin = _window(v_refs[3 * c:3 * c + 3], q0, rows_d, cols, RADIUS_A)
                (acc, den, mx), = _attend(_scores([qblk], kwin, bias_ref, c, variant), 1, vwin)
                if d == 1:
                    rows = pl.ds(q0, Q_BLOCK)
                elif d == MERGE_STRIDE:
                    rows = pl.ds(r * slab + q0, Q_BLOCK)
                else:
                    sub = d // MERGE_STRIDE
                    rows = pl.ds((r % MERGE_STRIDE) * slab + r // MERGE_STRIDE + sub * q0, Q_BLOCK,
                                 stride=sub)
                acc_ref[c, rows, :] = acc
                den_ref[c, rows, :] = den
                max_ref[c, rows, :] = mx

    chunk = 256
    for i0 in range(0, slab, chunk):
        for r in range(MERGE_STRIDE):
            token_rows = pl.ds(r + MERGE_STRIDE * i0, chunk, stride=MERGE_STRIDE)
            slab_rows = pl.ds(r * slab + i0, chunk)
            rows = [token_rows if d == 1 else slab_rows for d in DILATIONS]
            ms = [max_ref[c, rows[c], :] for c in range(nd)]
            top = functools.reduce(jnp.maximum, ms)
            num = None
            den = None
            for c in range(nd):
                w = jnp.exp2(ms[c] - top)
                n_c = w * acc_ref[c, rows[c], :]
                d_c = w * den_ref[c, rows[c], :]
                num = n_c if num is None else num + n_c
                den = d_c if den is None else den + d_c
            o_ref[token_rows, :] = num / den


def _halo_specs(d, rows, total_rows, radius, lead_map):
    per = rows // radius
    last = total_rows // radius - 1
    width = d * PAIR

    def prev(*g):
        return (*lead_map(*g), jnp.maximum(g[-1] * per - 1, 0), 0)

    def cur(*g):
        return (*lead_map(*g), g[-1], 0)

    def nxt(*g):
        return (*lead_map(*g), jnp.minimum((g[-1] + 1) * per, last), 0)

    return [pl.BlockSpec((None, None, radius, width), prev),
            pl.BlockSpec((None, None, rows, width), cur),
            pl.BlockSpec((None, None, radius, width), nxt)]


def _mixer_a(qa, ka, va, slopes, batch, seq):
    t = TOKEN_TILE_ATTN
    n_tiles = seq // t
    lead = lambda hp, b, i: (hp, b)
    views = lambda a: [v.reshape(N_PAIRS, batch, seq // d, d * PAIR) for v, d in zip(a, DILATIONS)]
    in_specs = [pl.BlockSpec(memory_space=pltpu.SMEM)]
    for d in DILATIONS:
        in_specs.append(pl.BlockSpec((None, None, t // d, d * PAIR), lambda hp, b, i: (hp, b, i, 0)))
    halo = []
    for d in DILATIONS:
        halo += _halo_specs(d, t // d, seq // d, RADIUS_A, lead)
    in_specs += halo + halo
    k_views = [v for kv in views(ka) for v in (kv, kv, kv)]
    v_views = [v for vv in views(va) for v in (vv, vv, vv)]
    scratch = [pltpu.VMEM((len(DILATIONS), 4, 2 * Q_BLOCK, Q_BLOCK + 2 * RADIUS_A), F32)]
    scratch += [pltpu.VMEM((len(DILATIONS), t, PAIR), F32)] * 3
    return pl.pallas_call(
        functools.partial(_mixer_a_kernel, n_tiles=n_tiles),
        grid=(N_PAIRS, batch, n_tiles),
        in_specs=in_specs,
        out_specs=pl.BlockSpec((None, t, PAIR), lambda hp, b, i: (b, i, hp)),
        out_shape=jax.ShapeDtypeStruct((batch, seq, WIDTH_A), F32),
        scratch_shapes=scratch,
        compiler_params=pltpu.CompilerParams(
            dimension_semantics=("arbitrary", "arbitrary", "arbitrary"), vmem_limit_bytes=VMEM_LIMIT),
        name="mixer_a",
    )(slopes, *views(qa), *k_views, *v_views)


def _mixer_b_kernel(slopes_ref, sink_ref, q0_ref, q1_ref, kp_ref, kc_ref, kn_ref, vp_ref, vc_ref, vn_ref,
                    o_ref, bias_ref, s_ref, *, n_tiles):
    g = pl.program_id(0)
    b = pl.program_id(1)
    i = pl.program_id(2)
    t = TOKEN_TILE_ATTN
    n_blocks = t // Q_BLOCK
    heads = [4 * g + h for h in range(4)]

    @pl.when((b == 0) & (i == 0))
    def _():
        _build_bias(bias_ref, 0, [slopes_ref[h] for h in heads], RADIUS_B, 1.0)

    sinks = [sink_ref[h] * LOG2E for h in heads]
    lane = lax.broadcasted_iota(jnp.int32, (1, PAIR), 1)
    every = slice(None)

    def scores(qb):
        q0 = qb * Q_BLOCK
        variant = _edge_variant(qb, n_blocks, i, n_tiles)
        qblks = [q_ref[q0:q0 + Q_BLOCK, :] for q_ref in (q0_ref, q1_ref)]
        kwin = _window((kp_ref, kc_ref, kn_ref), q0, t, every, RADIUS_B)
        s_ref[qb % (SCORE_AHEAD + 1)] = _scores(qblks, kwin, bias_ref, 0, variant)

    for qb in range(SCORE_AHEAD):
        scores(qb)
    for qb in range(n_blocks):
        q0 = qb * Q_BLOCK
        if qb + SCORE_AHEAD < n_blocks:
            scores(qb + SCORE_AHEAD)
        vwin = _window((vp_ref, vc_ref, vn_ref), q0, t, every, RADIUS_B)
        stats = _attend(s_ref[qb % (SCORE_AHEAD + 1)], 2, vwin, sinks=sinks)
        for k, (acc, den, mx) in enumerate(stats):
            sink_pair = jnp.where(lane < HEAD_DIM, sinks[2 * k], sinks[2 * k + 1])
            den = den + jnp.exp2(sink_pair - mx)
            o_ref[q0:q0 + Q_BLOCK, k * PAIR:(k + 1) * PAIR] = acc / den


def _mixer_b(qb, kb, vb, slopes, sinks, batch, seq):
    t = TOKEN_TILE_ATTN
    n_tiles = seq // t
    kv_lead = lambda g, b, i: (g, b)
    view = lambda a: a.reshape(a.shape[0], batch, seq, PAIR)
    halo = _halo_specs(1, t, seq, RADIUS_B, kv_lead)
    smem = pl.BlockSpec(memory_space=pltpu.SMEM)
    q_spec = lambda k: pl.BlockSpec((None, None, t, PAIR), lambda g, b, i: (2 * g + k, b, i, 0))
    return pl.pallas_call(
        functools.partial(_mixer_b_kernel, n_tiles=n_tiles),
        grid=(2, batch, n_tiles),
        in_specs=[smem, smem, q_spec(0), q_spec(1)] + halo + halo,
        out_specs=pl.BlockSpec((None, t, 2 * PAIR), lambda g, b, i: (b, i, g)),
        out_shape=jax.ShapeDtypeStruct((batch, seq, WIDTH_B), F32),
        scratch_shapes=[pltpu.VMEM((1, 4, 4 * Q_BLOCK, Q_BLOCK + 2 * RADIUS_B), F32),
                        pltpu.VMEM((SCORE_AHEAD + 1, 4 * Q_BLOCK, Q_BLOCK + 2 * RADIUS_B), F32)],
        compiler_params=pltpu.CompilerParams(
            dimension_semantics=("arbitrary", "arbitrary", "arbitrary"), vmem_limit_bytes=VMEM_LIMIT),
        name="mixer_b",
    )(slopes, sinks, view(qb), view(qb), *([view(kb)] * 3), *([view(vb)] * 3))


def _rms(x, g):
    ms = jnp.mean(x * x, axis=-1, keepdims=True)
    return (x * lax.rsqrt(ms + EPS)) * g


def _outffn_kernel(yap_ref, yac_ref, yan_ref, ybp_ref, ybc_ref, ybn_ref, xp_ref, xc_ref, xn_ref,
                   ga_ref, gb_ref, wo_ref, n2_ref, wu_ref, cw_ref, cb_ref,
                   wd_ref, o_ref, y_ref, h2_ref, perm_ref, hid_ref, *, tiles_per_seq):
    i = pl.program_id(0)
    t = TOKEN_TILE_FFN
    halo = FFN_HALO
    first = (i % tiles_per_seq) == 0
    last = (i % tiles_per_seq) == tiles_per_seq - 1

    for r0, ya_ref, yb_ref in ((0, yap_ref, ybp_ref), (halo, yac_ref, ybc_ref), (halo + t, yan_ref, ybn_ref)):
        n = ya_ref.shape[0]
        y_ref[r0:r0 + n, 0:WIDTH_A] = _rms(ya_ref[...], ga_ref[...]).astype(BF16)
        y_ref[r0:r0 + n, WIDTH_A:] = _rms(yb_ref[...], gb_ref[...]).astype(BF16)
    mixed = jnp.dot(y_ref[...], wo_ref[...], preferred_element_type=F32)

    h_prev = jnp.where(first, 0.0, _rms(xp_ref[...] + mixed[0:halo], n2_ref[...]))
    h_next = jnp.where(last, 0.0, _rms(xn_ref[...] + mixed[halo + t:], n2_ref[...]))
    x1 = xc_ref[...] + mixed[halo:halo + t]
    o_ref[...] = x1

    per = t // 8
    h_cur = _rms(x1, n2_ref[...])
    lane_tiles = D_MODEL // 128
    for c in range(lane_tiles):
        for s in range(8):
            perm_ref[c, pl.ds(s, per, stride=8), :] = h_cur[s * per:(s + 1) * per, c * 128:(c + 1) * 128]
        h2_ref[0:t, c * 128:(c + 1) * 128] = perm_ref[c].astype(BF16)
    h2_ref[t:, :] = jnp.concatenate([h_prev[halo - 8:], h_next[:8]], axis=0).astype(BF16)

    sub = lax.broadcasted_iota(jnp.int32, (8, FF_CHUNK), 0)

    def conv(u, w, b):
        body, edge = u[0:t], u[t:]
        lo = jnp.where(sub == 0, pltpu.roll(edge[0:8], 1, axis=0), pltpu.roll(body[t - 8:], 1, axis=0))
        hi = jnp.where(sub == 7, pltpu.roll(edge[8:16], 7, axis=0), pltpu.roll(body[0:8], 7, axis=0))
        below = jnp.concatenate([lo, body[:t - 8]], axis=0)
        above = jnp.concatenate([body[8:], hi], axis=0)
        return b + below * w[0:1] + body * w[1:2] + above * w[2:3]

    for j in range(N_FF_CHUNKS):
        h2 = h2_ref[...]
        branch = []
        for c0 in (j * FF_CHUNK, D_FF + j * FF_CHUNK):
            cols = slice(c0, c0 + FF_CHUNK)
            u = jnp.dot(h2, wu_ref[:, cols], preferred_element_type=F32)
            branch.append(conv(u, cw_ref[:, cols], cb_ref[:, cols]))
        gate, val = branch
        hid_ref[:, j * FF_CHUNK:(j + 1) * FF_CHUNK] = (jax.nn.silu(gate) * val).astype(BF16)

    down = jnp.dot(hid_ref[...], wd_ref[...], preferred_element_type=F32)
    for c in range(lane_tiles):
        perm_ref[c] = down[:, c * 128:(c + 1) * 128]
        for s in range(8):
            o_ref[s * per:(s + 1) * per, c * 128:(c + 1) * 128] += perm_ref[c, pl.ds(s, per, stride=8), :]


def _outffn(ya, yb, x2, ga, gb, w_out, n2, wu, cw, cb, wd, seq):
    n = x2.shape[0]
    t = TOKEN_TILE_FFN
    per = t // FFN_HALO
    last = n // FFN_HALO - 1
    prev = lambda i: (jnp.maximum(i * per - 1, 0), 0)
    cur = lambda i: (i, 0)
    nxt = lambda i: (jnp.minimum((i + 1) * per, last), 0)

    def halo_specs(width):
        return [pl.BlockSpec((FFN_HALO, width), prev), pl.BlockSpec((t, width), cur),
                pl.BlockSpec((FFN_HALO, width), nxt)]

    def resident(shape):
        return pl.BlockSpec(shape, lambda i: (0,) * len(shape), pipeline_mode=pl.Buffered(1))

    return pl.pallas_call(
        functools.partial(_outffn_kernel, tiles_per_seq=seq // t),
        grid=(n // t,),
        in_specs=halo_specs(WIDTH_A) + halo_specs(WIDTH_B) + halo_specs(D_MODEL) + [
            resident((1, WIDTH_A)), resident((1, WIDTH_B)), resident((D_MODEL, D_MODEL)),
            resident((1, D_MODEL)),
            resident((D_MODEL, 2 * D_FF)), resident((3, 2 * D_FF)), resident((1, 2 * D_FF)),
            resident((D_FF, D_MODEL)),
        ],
        out_specs=pl.BlockSpec((t, D_MODEL), cur),
        out_shape=jax.ShapeDtypeStruct((n, D_MODEL), F32),
        scratch_shapes=[pltpu.VMEM((t + 2 * FFN_HALO, D_MODEL), BF16),
                        pltpu.VMEM((t + FFN_HALO, D_MODEL), BF16),
                        pltpu.VMEM((D_MODEL // 128, t, 128), F32),
                        pltpu.VMEM((t, D_FF), BF16)],
        compiler_params=pltpu.CompilerParams(
            dimension_semantics=("arbitrary",), vmem_limit_bytes=VMEM_LIMIT),
        name="outffn",
    )(ya, ya, ya, yb, yb, yb, x2, x2, x2, ga, gb, w_out, n2, wu, cw, cb, wd)


def _prepare(norm1, w_in, q_norm_a, k_norm_a, q_norm_b, k_norm_b, sink_b, out_norm_a, out_norm_b, w_out,
             norm2, w_up, conv_w, conv_b, w_down):
    scale = HEAD_DIM ** -0.5 * LOG2E
    lanes = np.arange(256)
    src = np.arange(PAIR)
    rep = ((src[:, None] // HEAD_DIM == lanes[None, :] // PAIR)
           & (src[:, None] % HEAD_DIM == lanes[None, :] % HEAD_DIM)).astype(np.float32)
    slopes_a, slopes_b = _alibi_slopes()

    return dict(
        n1=norm1.reshape(1, D_MODEL),
        w_in=w_in.astype(BF16),
        rep=jnp.asarray(rep, BF16),
        gqa=(jnp.tile(q_norm_a, 4) * scale).reshape(1, 256),
        gka=jnp.tile(k_norm_a, 4).reshape(1, 256),
        gqb=(jnp.tile(q_norm_b, 4) * scale).reshape(1, 256),
        gkb=jnp.tile(k_norm_b, 2).reshape(1, PAIR),
        slopes_a=jnp.asarray(slopes_a),
        slopes_b=jnp.asarray(slopes_b),
        sinks=sink_b.astype(F32),
        ga=out_norm_a.reshape(1, WIDTH_A),
        gb=out_norm_b.reshape(1, WIDTH_B),
        w_out=w_out.astype(BF16),
        n2=norm2.reshape(1, D_MODEL),
        wu=w_up.astype(BF16),
        cw=conv_w,
        cb=conv_b.reshape(1, 2 * D_FF),
        wd=w_down.astype(BF16),
    )


def _layer(x, p):
    batch, seq, _ = x.shape
    assert seq % TOKEN_TILE_ATTN == 0 and seq % TOKEN_TILE_FFN == 0
    x2 = x.reshape(batch * seq, D_MODEL)
    qa, ka, va, qb, kb, vb = _inproj(x2, p["n1"], p["w_in"], p["rep"],
                                     p["gqa"], p["gka"], p["gqb"], p["gkb"])
    ya = _mixer_a(qa, ka, va, p["slopes_a"], batch, seq).reshape(batch * seq, WIDTH_A)
    yb = _mixer_b(qb, kb, vb, p["slopes_b"], p["sinks"], batch, seq).reshape(batch * seq, WIDTH_B)
    out = _outffn(ya, yb, x2, p["ga"], p["gb"], p["w_out"], p["n2"], p["wu"], p["cw"], p["cb"],
                  p["wd"], seq)
    return out.reshape(batch, seq, D_MODEL)


def kernel(x_prompt, x_sample, norm1, w_in, q_norm_a, k_norm_a, q_norm_b, k_norm_b, sink_b, out_norm_a,
           out_norm_b, w_out, norm2, w_up, conv_w, conv_b, w_down):
    y_prompt, y_sample = x_prompt, x_sample
    for l in range(norm1.shape[0]):
        p = _prepare(norm1[l], w_in[l], q_norm_a[l], k_norm_a[l], q_norm_b[l], k_norm_b[l], sink_b[l],
                     out_norm_a[l], out_norm_b[l], w_out[l], norm2[l], w_up[l], conv_w[l], conv_b[l],
                     w_down[l])
        y_prompt = _layer(y_prompt, p)
        y_sample = _layer(y_sample, p)
    return (y_prompt, y_sample)
```

```python
import functools

import numpy as np
import jax
import jax.numpy as jnp
from jax import lax
from jax.experimental import pallas as pl
from jax.experimental.pallas import tpu as pltpu

F32 = jnp.float32
BF16 = jnp.bfloat16

D_MODEL = 1024
HEAD_DIM = 64
PAIR = 2 * HEAD_DIM
N_PAIRS = 4
WIDTH_A = 512
WIDTH_B = 512
KV_WIDTH_B = 128
IN_WIDTH = 2304
DILATIONS = (1, 4, 16)
MERGE_STRIDE = 4
RADIUS_A = 64
RADIUS_B = 128
D_FF = 2816
EPS = 1e-6
MASKED = 1e30
LOG2E = 1.4426950408889634

Q_BLOCK = 128
SCORE_AHEAD = 2
TOKEN_TILE_IN = 1024
TOKEN_TILE_ATTN = 2048
TOKEN_TILE_FFN = 512
FFN_HALO = 16
FF_CHUNK = 256
N_FF_CHUNKS = D_FF // FF_CHUNK
VMEM_LIMIT = 56 * 1024 * 1024


def _alibi_slopes():
    n = 16
    h = np.arange(1, n + 1, dtype=np.float32)
    s = np.exp2(-8.0 * h / n).astype(np.float32)
    return s[8:], s[:8]


def _inproj_kernel(x_ref, n1_ref, w_ref, rep_ref, gqa_ref, gka_ref, gqb_ref, gkb_ref,
                   qa1_ref, qa4_ref, qa16_ref, ka1_ref, ka4_ref, ka16_ref, va1_ref, va4_ref, va16_ref,
                   qb_ref, kb_ref, vb_ref, stage_ref, stage4_ref):
    t = TOKEN_TILE_IN
    x = x_ref[...]
    ms = jnp.mean(x * x, axis=-1, keepdims=True)
    h = ((x * lax.rsqrt(ms + EPS)) * n1_ref[...]).astype(BF16)
    lane = lax.broadcasted_iota(jnp.int32, (1, PAIR), 1)
    head0 = lane < HEAD_DIM

    def proj(c0, n):
        return jnp.dot(h, w_ref[:, c0:c0 + n], preferred_element_type=F32)

    def head_norm(p, g):
        slabs = []
        for c in range(0, p.shape[1], PAIR):
            sq = p[:, c:c + PAIR] * p[:, c:c + PAIR]
            lo = jnp.sum(jnp.where(head0, sq, 0.0), axis=-1, keepdims=True)
            hi = jnp.sum(jnp.where(head0, 0.0, sq), axis=-1, keepdims=True)
            slabs.append(jnp.where(head0, lo, hi) * (1.0 / HEAD_DIM))
        msq = slabs[0] if len(slabs) == 1 else jnp.concatenate(slabs, axis=1)
        return (p * lax.rsqrt(msq + EPS)) * g

    def write_pairs(ref, y, first_pair):
        for j in range(y.shape[1] // PAIR):
            ref[first_pair + j] = y[:, j * PAIR:(j + 1) * PAIR].astype(BF16)

    def write_dilated(refs, y, first_pair, first_slot):
        ref1, ref4, ref16 = refs
        _, d4, d16 = DILATIONS
        sub = d16 // d4
        for j in range(y.shape[1] // PAIR):
            pair = first_pair + j
            yp = y[:, j * PAIR:(j + 1) * PAIR]
            ref1[pair] = yp.astype(BF16)
            stage = stage_ref.at[first_slot + j]
            stage4 = stage4_ref.at[first_slot + j]
            stage[...] = yp
            for r in range(d4):
                rows = stage[pl.ds(r, t // d4, stride=d4), :]
                stage4[r] = rows
                ref4[pair, :, r * PAIR:(r + 1) * PAIR] = rows.astype(BF16)
            for b in range(d16):
                rows = stage4[b % d4, pl.ds(b // d4, t // d16, stride=sub), :]
                ref16[pair, :, b * PAIR:(b + 1) * PAIR] = rows.astype(BF16)

    qa_refs = (qa1_ref, qa4_ref, qa16_ref)
    ka_refs = (ka1_ref, ka4_ref, ka16_ref)
    va_refs = (va1_ref, va4_ref, va16_ref)

    def head_norm_wide(p, g):
        halves = [head_norm(p[:, c:c + 256], g) for c in (0, 256)]
        return jnp.concatenate(halves, axis=1)

    write_dilated(qa_refs, head_norm_wide(proj(0, 512), gqa_ref[...]), 0, 0)
    write_dilated(ka_refs, head_norm_wide(proj(512, 512), gka_ref[...]), 0, 4)
    write_dilated(va_refs, proj(1024, 512), 0, 8)
    write_pairs(qb_ref, head_norm_wide(proj(1536, 512), gqb_ref[...]), 0)

    rep = rep_ref[...]
    kvb = proj(2048, 2 * KV_WIDTH_B)
    kb = head_norm(kvb[:, :KV_WIDTH_B], gkb_ref[...]).astype(BF16)
    vb = kvb[:, KV_WIDTH_B:].astype(BF16)
    write_pairs(kb_ref, jnp.dot(kb, rep, preferred_element_type=F32), 0)
    write_pairs(vb_ref, jnp.dot(vb, rep, preferred_element_type=F32), 0)


def _inproj(x2, n1, w_in, rep, gqa, gka, gqb, gkb):
    n = x2.shape[0]
    t = TOKEN_TILE_IN
    const = lambda i: (0, 0)
    pair_out = lambda p, d=1: pl.BlockSpec((p, t // d, d * PAIR), lambda i: (0, i, 0))
    pair_shape = lambda p, d=1: jax.ShapeDtypeStruct((p, n // d, d * PAIR), BF16)
    dilated = [(N_PAIRS, d) for _ in range(3) for d in DILATIONS]
    outs = dilated + [(N_PAIRS, 1), (2, 1), (2, 1)]
    res = pl.pallas_call(
        _inproj_kernel,
        grid=(n // t,),
        in_specs=[
            pl.BlockSpec((t, D_MODEL), lambda i: (i, 0)),
            pl.BlockSpec((1, D_MODEL), const),
            pl.BlockSpec((D_MODEL, IN_WIDTH), const, pipeline_mode=pl.Buffered(1)),
            pl.BlockSpec((PAIR, 256), const),
            pl.BlockSpec((1, 256), const),
            pl.BlockSpec((1, 256), const),
            pl.BlockSpec((1, 256), const),
            pl.BlockSpec((1, PAIR), const),
        ],
        out_specs=[pair_out(p, d) for p, d in outs],
        out_shape=[pair_shape(p, d) for p, d in outs],
        scratch_shapes=[pltpu.VMEM((3 * N_PAIRS, t, PAIR), F32),
                        pltpu.VMEM((3 * N_PAIRS, DILATIONS[1], t // DILATIONS[1], PAIR), F32)],
        compiler_params=pltpu.CompilerParams(
            dimension_semantics=("arbitrary",), vmem_limit_bytes=VMEM_LIMIT),
        name="inproj",
    )(x2, n1, w_in, rep, gqa, gka, gqb, gkb)
    return res[0:3], res[3:6], res[6:9], res[9], res[10], res[11]


def _build_bias(bias_ref, cfg, slopes, radius, dist_scale):
    width = Q_BLOCK + 2 * radius
    row = lax.broadcasted_iota(jnp.int32, (Q_BLOCK, width), 0)
    col = lax.broadcasted_iota(jnp.int32, (Q_BLOCK, width), 1)
    rel = col - radius - row
    band = jnp.abs(rel) <= radius
    dist = jnp.abs(rel).astype(F32) * dist_scale
    for variant in range(4):
        ok = band
        if variant & 1:
            ok = ok & (col >= radius)
        if variant & 2:
            ok = ok & (col < Q_BLOCK + radius)
        base = jnp.where(ok, dist, MASKED)
        for h, slope in enumerate(slopes):
            bias_ref[cfg, variant, h * Q_BLOCK:(h + 1) * Q_BLOCK] = base * (slope * LOG2E)


def _edge_variant(qb, n_blocks, i, n_tiles):
    variant = 0
    if qb == 0:
        variant = variant + (i == 0).astype(jnp.int32)
    if qb == n_blocks - 1:
        variant = variant + 2 * (i == n_tiles - 1).astype(jnp.int32)
    return variant


def _window(refs, q0, rows, cols, radius):
    prev_ref, cur_ref, next_ref = refs
    lo, hi = q0 - radius, q0 + Q_BLOCK + radius
    parts = []
    if lo < 0:
        parts.append(prev_ref[:, cols])
    parts.append(cur_ref[max(lo, 0):min(hi, rows), cols])
    if hi > rows:
        parts.append(next_ref[:, cols])
    return parts[0] if len(parts) == 1 else jnp.concatenate(parts, axis=0)


def _scores(qblks, kwin, bias_ref, cfg, variant):
    lane = lax.broadcasted_iota(jnp.int32, (1, PAIR), 1)
    head0 = lane < HEAD_DIM
    keep = (head0.astype(BF16), (~head0).astype(BF16))
    q = jnp.concatenate([qblk * k for qblk in qblks for k in keep], axis=0)
    s = lax.dot_general(q, kwin, (((1,), (1,)), ((), ())), preferred_element_type=F32)
    return s - bias_ref[cfg, variant]


def _attend(s, n_pairs, vwin, sinks=None):
    lane = lax.broadcasted_iota(jnp.int32, (1, PAIR), 1)
    head0 = lane < HEAD_DIM
    m = jnp.max(s, axis=-1, keepdims=True)
    if sinks is not None:
        m = jnp.maximum(m, jnp.concatenate([jnp.full((Q_BLOCK, 1), sk, F32) for sk in sinks], axis=0))
    p = jnp.exp2(s - m).astype(BF16)
    rhs = jnp.concatenate([vwin, jnp.ones_like(vwin)], axis=1)
    o = jnp.dot(p, rhs, preferred_element_type=F32)
    m_b = jnp.broadcast_to(m, (s.shape[0], PAIR))
    out = []
    for k in range(n_pairs):
        top = slice(2 * k * Q_BLOCK, (2 * k + 1) * Q_BLOCK)
        bot = slice((2 * k + 1) * Q_BLOCK, (2 * k + 2) * Q_BLOCK)
        pick = lambda a, top=top, bot=bot: jnp.where(head0, a[top], a[bot])
        out.append((pick(o[:, :PAIR]), pick(o[:, PAIR:]), pick(m_b)))
    return out


def _mixer_a_kernel(slopes_ref, *refs, n_tiles):
    nd = len(DILATIONS)
    q_refs = refs[0:nd]
    k_refs = refs[nd:4 * nd]
    v_refs = refs[4 * nd:7 * nd]
    o_ref = refs[7 * nd]
    bias_ref, acc_ref, den_ref, max_ref = refs[7 * nd + 1:]

    hp = pl.program_id(0)
    b = pl.program_id(1)
    i = pl.program_id(2)
    t = TOKEN_TILE_ATTN
    slab = t // MERGE_STRIDE

    @pl.when((b == 0) & (i == 0))
    def _():
        for c, d in enumerate(DILATIONS):
            _build_bias(bias_ref, c, (slopes_ref[2 * hp], slopes_ref[2 * hp + 1]), RADIUS_A, float(d))

    for c, d in reversed(list(enumerate(DILATIONS))):
        rows_d = t // d
        n_blocks = rows_d // Q_BLOCK
        for qb in range(n_blocks):
            q0 = qb * Q_BLOCK
            variant = _edge_variant(qb, n_blocks, i, n_tiles)
            for r in range(d):
                cols = slice(r * PAIR, (r + 1) * PAIR)
                qblk = q_refs[c][q0:q0 + Q_BLOCK, cols]
                kwin = _window(k_refs[3 * c:3 * c + 3], q0, rows_d, cols, RADIUS_A)
                vwin = _window(v_refs[3 * c:3 * c + 3], q0, rows_d, cols, RADIUS_A)
                (acc, den, mx), = _attend(_scores([qblk], kwin, bias_ref, c, variant), 1, vwin)
                if d == 1:
                    rows = pl.ds(q0, Q_BLOCK)
                elif d == MERGE_STRIDE:
                    rows = pl.ds(r * slab + q0, Q_BLOCK)
                else:
                    sub = d // MERGE_STRIDE
                    rows = pl.ds((r % MERGE_STRIDE) * slab + r // MERGE_STRIDE + sub * q0, Q_BLOCK,
                                 stride=sub)
                acc_ref[c, rows, :] = acc
                den_ref[c, rows, :] = den
                max_ref[c, rows, :] = mx

    chunk = 256
    for i0 in range(0, slab, chunk):
        for r in range(MERGE_STRIDE):
            token_rows = pl.ds(r + MERGE_STRIDE * i0, chunk, stride=MERGE_STRIDE)
            slab_rows = pl.ds(r * slab + i0, chunk)
            rows = [token_rows if d == 1 else slab_rows for d in DILATIONS]
            ms = [max_ref[c, rows[c], :] for c in range(nd)]
            top = functools.reduce(jnp.maximum, ms)
            num = None
            den = None
            for c in range(nd):
                w = jnp.exp2(ms[c] - top)
                n_c = w * acc_ref[c, rows[c], :]
                d_c = w * den_ref[c, rows[c], :]
                num = n_c if num is None else num + n_c
                den = d_c if den is None else den + d_c
            o_ref[token_rows, :] = num / den


def _halo_specs(d, rows, total_rows, radius, lead_map):
    per = rows // radius
    last = total_rows // radius - 1
    width = d * PAIR

    def prev(*g):
        return (*lead_map(*g), jnp.maximum(g[-1] * per - 1, 0), 0)

    def cur(*g):
        return (*lead_map(*g), g[-1], 0)

    def nxt(*g):
        return (*lead_map(*g), jnp.minimum((g[-1] + 1) * per, last), 0)

    return [pl.BlockSpec((None, None, radius, width), prev),
            pl.BlockSpec((None, None, rows, width), cur),
            pl.BlockSpec((None, None, radius, width), nxt)]


def _mixer_a(qa, ka, va, slopes, batch, seq):
    t = TOKEN_TILE_ATTN
    n_tiles = seq // t
    lead = lambda hp, b, i: (hp, b)
    views = lambda a: [v.reshape(N_PAIRS, batch, seq // d, d * PAIR) for v, d in zip(a, DILATIONS)]
    in_specs = [pl.BlockSpec(memory_space=pltpu.SMEM)]
    for d in DILATIONS:
        in_specs.append(pl.BlockSpec((None, None, t // d, d * PAIR), lambda hp, b, i: (hp, b, i, 0)))
    halo = []
    for d in DILATIONS:
        halo += _halo_specs(d, t // d, seq // d, RADIUS_A, lead)
    in_specs += halo + halo
    k_views = [v for kv in views(ka) for v in (kv, kv, kv)]
    v_views = [v for vv in views(va) for v in (vv, vv, vv)]
    scratch = [pltpu.VMEM((len(DILATIONS), 4, 2 * Q_BLOCK, Q_BLOCK + 2 * RADIUS_A), F32)]
    scratch += [pltpu.VMEM((len(DILATIONS), t, PAIR), F32)] * 3
    return pl.pallas_call(
        functools.partial(_mixer_a_kernel, n_tiles=n_tiles),
        grid=(N_PAIRS, batch, n_tiles),
        in_specs=in_specs,
        out_specs=pl.BlockSpec((None, t, PAIR), lambda hp, b, i: (b, i, hp)),
        out_shape=jax.ShapeDtypeStruct((batch, seq, WIDTH_A), F32),
        scratch_shapes=scratch,
        compiler_params=pltpu.CompilerParams(
            dimension_semantics=("arbitrary", "arbitrary", "arbitrary"), vmem_limit_bytes=VMEM_LIMIT),
        name="mixer_a",
    )(slopes, *views(qa), *k_views, *v_views)


def _mixer_b_kernel(slopes_ref, sink_ref, q0_ref, q1_ref, kp_ref, kc_ref, kn_ref, vp_ref, vc_ref, vn_ref,
                    o_ref, bias_ref, s_ref, *, n_tiles):
    g = pl.program_id(0)
    b = pl.program_id(1)
    i = pl.program_id(2)
    t = TOKEN_TILE_ATTN
    n_blocks = t // Q_BLOCK
    heads = [4 * g + h for h in range(4)]

    @pl.when((b == 0) & (i == 0))
    def _():
        _build_bias(bias_ref, 0, [slopes_ref[h] for h in heads], RADIUS_B, 1.0)

    sinks = [sink_ref[h] * LOG2E for h in heads]
    lane = lax.broadcasted_iota(jnp.int32, (1, PAIR), 1)
    every = slice(None)

    def scores(qb):
        q0 = qb * Q_BLOCK
        variant = _edge_variant(qb, n_blocks, i, n_tiles)
        qblks = [q_ref[q0:q0 + Q_BLOCK, :] for q_ref in (q0_ref, q1_ref)]
        kwin = _window((kp_ref, kc_ref, kn_ref), q0, t, every, RADIUS_B)
        s_ref[qb % (SCORE_AHEAD + 1)] = _scores(qblks, kwin, bias_ref, 0, variant)

    for qb in range(SCORE_AHEAD):
        scores(qb)
    for qb in range(n_blocks):
        q0 = qb * Q_BLOCK
        if qb + SCORE_AHEAD < n_blocks:
            scores(qb + SCORE_AHEAD)
        vwin = _window((vp_ref, vc_ref, vn_ref), q0, t, every, RADIUS_B)
        stats = _attend(s_ref[qb % (SCORE_AHEAD + 1)], 2, vwin, sinks=sinks)
        for k, (acc, den, mx) in enumerate(stats):
            sink_pair = jnp.where(lane < HEAD_DIM, sinks[2 * k], sinks[2 * k + 1])
            den = den + jnp.exp2(sink_pair - mx)
            o_ref[q0:q0 + Q_BLOCK, k * PAIR:(k + 1) * PAIR] = acc / den


def _mixer_b(qb, kb, vb, slopes, sinks, batch, seq):
    t = TOKEN_TILE_ATTN
    n_tiles = seq // t
    kv_lead = lambda g, b, i: (g, b)
    view = lambda a: a.reshape(a.shape[0], batch, seq, PAIR)
    halo = _halo_specs(1, t, seq, RADIUS_B, kv_lead)
    smem = pl.BlockSpec(memory_space=pltpu.SMEM)
    q_spec = lambda k: pl.BlockSpec((None, None, t, PAIR), lambda g, b, i: (2 * g + k, b, i, 0))
    return pl.pallas_call(
        functools.partial(_mixer_b_kernel, n_tiles=n_tiles),
        grid=(2, batch, n_tiles),
        in_specs=[smem, smem, q_spec(0), q_spec(1)] + halo + halo,
        out_specs=pl.BlockSpec((None, t, 2 * PAIR), lambda g, b, i: (b, i, g)),
        out_shape=jax.ShapeDtypeStruct((batch, seq, WIDTH_B), F32),
        scratch_shapes=[pltpu.VMEM((1, 4, 4 * Q_BLOCK, Q_BLOCK + 2 * RADIUS_B), F32),
                        pltpu.VMEM((SCORE_AHEAD + 1, 4 * Q_BLOCK, Q_BLOCK + 2 * RADIUS_B), F32)],
        compiler_params=pltpu.CompilerParams(
            dimension_semantics=("arbitrary", "arbitrary", "arbitrary"), vmem_limit_bytes=VMEM_LIMIT),
        name="mixer_b",
    )(slopes, sinks, view(qb), view(qb), *([view(kb)] * 3), *([view(vb)] * 3))


def _rms(x, g):
    ms = jnp.mean(x * x, axis=-1, keepdims=True)
    return (x * lax.rsqrt(ms + EPS)) * g


def _outffn_kernel(yap_ref, yac_ref, yan_ref, ybp_ref, ybc_ref, ybn_ref, xp_ref, xc_ref, xn_ref,
                   ga_ref, gb_ref, wo_ref, n2_ref, wu_ref, cw_ref, cb_ref,
                   wd_ref, o_ref, y_ref, h2_ref, perm_ref, hid_ref, *, tiles_per_seq):
    i = pl.program_id(0)
    t = TOKEN_TILE_FFN
    halo = FFN_HALO
    first = (i % tiles_per_seq) == 0
    last = (i % tiles_per_seq) == tiles_per_seq - 1

    for r0, ya_ref, yb_ref in ((0, yap_ref, ybp_ref), (halo, yac_ref, ybc_ref), (halo + t, yan_ref, ybn_ref)):
        n = ya_ref.shape[0]
        y_ref[r0:r0 + n, 0:WIDTH_A] = _rms(ya_ref[...], ga_ref[...]).astype(BF16)
        y_ref[r0:r0 + n, WIDTH_A:] = _rms(yb_ref[...], gb_ref[...]).astype(BF16)
    mixed = jnp.dot(y_ref[...], wo_ref[...], preferred_element_type=F32)

    h_prev = jnp.where(first, 0.0, _rms(xp_ref[...] + mixed[0:halo], n2_ref[...]))
    h_next = jnp.where(last, 0.0, _rms(xn_ref[...] + mixed[halo + t:], n2_ref[...]))
    x1 = xc_ref[...] + mixed[halo:halo + t]
    o_ref[...] = x1

    per = t // 8
    h_cur = _rms(x1, n2_ref[...])
    lane_tiles = D_MODEL // 128
    for c in range(lane_tiles):
        for s in range(8):
            perm_ref[c, pl.ds(s, per, stride=8), :] = h_cur[s * per:(s + 1) * per, c * 128:(c + 1) * 128]
        h2_ref[0:t, c * 128:(c + 1) * 128] = perm_ref[c].astype(BF16)
    h2_ref[t:, :] = jnp.concatenate([h_prev[halo - 8:], h_next[:8]], axis=0).astype(BF16)

    sub = lax.broadcasted_iota(jnp.int32, (8, FF_CHUNK), 0)

    def conv(u, w, b):
        body, edge = u[0:t], u[t:]
        lo = jnp.where(sub == 0, pltpu.roll(edge[0:8], 1, axis=0), pltpu.roll(body[t - 8:], 1, axis=0))
        hi = jnp.where(sub == 7, pltpu.roll(edge[8:16], 7, axis=0), pltpu.roll(body[0:8], 7, axis=0))
        below = jnp.concatenate([lo, body[:t - 8]], axis=0)
        above = jnp.concatenate([body[8:], hi], axis=0)
        return b + below * w[0:1] + body * w[1:2] + above * w[2:3]

    for j in range(N_FF_CHUNKS):
        h2 = h2_ref[...]
        branch = []
        for c0 in (j * FF_CHUNK, D_FF + j * FF_CHUNK):
            cols = slice(c0, c0 + FF_CHUNK)
            u = jnp.dot(h2, wu_ref[:, cols], preferred_element_type=F32)
            branch.append(conv(u, cw_ref[:, cols], cb_ref[:, cols]))
        gate, val = branch
        hid_ref[:, j * FF_CHUNK:(j + 1) * FF_CHUNK] = (jax.nn.silu(gate) * val).astype(BF16)

    down = jnp.dot(hid_ref[...], wd_ref[...], preferred_element_type=F32)
    for c in range(lane_tiles):
        perm_ref[c] = down[:, c * 128:(c + 1) * 128]
        for s in range(8):
            o_ref[s * per:(s + 1) * per, c * 128:(c + 1) * 128] += perm_ref[c, pl.ds(s, per, stride=8), :]


def _outffn(ya, yb, x2, ga, gb, w_out, n2, wu, cw, cb, wd, seq):
    n = x2.shape[0]
    t = TOKEN_TILE_FFN
    per = t // FFN_HALO
    last = n // FFN_HALO - 1
    prev = lambda i: (jnp.maximum(i * per - 1, 0), 0)
    cur = lambda i: (i, 0)
    nxt = lambda i: (jnp.minimum((i + 1) * per, last), 0)

    def halo_specs(width):
        return [pl.BlockSpec((FFN_HALO, width), prev), pl.BlockSpec((t, width), cur),
                pl.BlockSpec((FFN_HALO, width), nxt)]

    def resident(shape):
        return pl.BlockSpec(shape, lambda i: (0,) * len(shape), pipeline_mode=pl.Buffered(1))

    return pl.pallas_call(
        functools.partial(_outffn_kernel, tiles_per_seq=seq // t),
        grid=(n // t,),
        in_specs=halo_specs(WIDTH_A) + halo_specs(WIDTH_B) + halo_specs(D_MODEL) + [
            resident((1, WIDTH_A)), resident((1, WIDTH_B)), resident((D_MODEL, D_MODEL)),
            resident((1, D_MODEL)),
            resident((D_MODEL, 2 * D_FF)), resident((3, 2 * D_FF)), resident((1, 2 * D_FF)),
            resident((D_FF, D_MODEL)),
        ],
        out_specs=pl.BlockSpec((t, D_MODEL), cur),
        out_shape=jax.ShapeDtypeStruct((n, D_MODEL), F32),
        scratch_shapes=[pltpu.VMEM((t + 2 * FFN_HALO, D_MODEL), BF16),
                        pltpu.VMEM((t + FFN_HALO, D_MODEL), BF16),
                        pltpu.VMEM((D_MODEL // 128, t, 128), F32),
                        pltpu.VMEM((t, D_FF), BF16)],
        compiler_params=pltpu.CompilerParams(
            dimension_semantics=("arbitrary",), vmem_limit_bytes=VMEM_LIMIT),
        name="outffn",
    )(ya, ya, ya, yb, yb, yb, x2, x2, x2, ga, gb, w_out, n2, wu, cw, cb, wd)


def _prepare(norm1, w_in, q_norm_a, k_norm_a, q_norm_b, k_norm_b, sink_b, out_norm_a, out_norm_b, w_out,
             norm2, w_up, conv_w, conv_b, w_down):
    scale = HEAD_DIM ** -0.5 * LOG2E
    lanes = np.arange(256)
    src = np.arange(PAIR)
    rep = ((src[:, None] // HEAD_DIM == lanes[None, :] // PAIR)
           & (src[:, None] % HEAD_DIM == lanes[None, :] % HEAD_DIM)).astype(np.float32)
    slopes_a, slopes_b = _alibi_slopes()

    return dict(
        n1=norm1.reshape(1, D_MODEL),
        w_in=w_in.astype(BF16),
        rep=jnp.asarray(rep, BF16),
        gqa=(jnp.tile(q_norm_a, 4) * scale).reshape(1, 256),
        gka=jnp.tile(k_norm_a, 4).reshape(1, 256),
        gqb=(jnp.tile(q_norm_b, 4) * scale).reshape(1, 256),
        gkb=jnp.tile(k_norm_b, 2).reshape(1, PAIR),
        slopes_a=jnp.asarray(slopes_a),
        slopes_b=jnp.asarray(slopes_b),
        sinks=sink_b.astype(F32),
        ga=out_norm_a.reshape(1, WIDTH_A),
        gb=out_norm_b.reshape(1, WIDTH_B),
        w_out=w_out.astype(BF16),
        n2=norm2.reshape(1, D_MODEL),
        wu=w_up.astype(BF16),
        cw=conv_w,
        cb=conv_b.reshape(1, 2 * D_FF),
        wd=w_down.astype(BF16),
    )


def _layer(x, p):
    batch, seq, _ = x.shape
    assert seq % TOKEN_TILE_ATTN == 0 and seq % TOKEN_TILE_FFN == 0
    x2 = x.reshape(batch * seq, D_MODEL)
    qa, ka, va, qb, kb, vb = _inproj(x2, p["n1"], p["w_in"], p["rep"],
                                     p["gqa"], p["gka"], p["gqb"], p["gkb"])
    ya = _mixer_a(qa, ka, va, p["slopes_a"], batch, seq).reshape(batch * seq, WIDTH_A)
    yb = _mixer_b(qb, kb, vb, p["slopes_b"], p["sinks"], batch, seq).reshape(batch * seq, WIDTH_B)
    out = _outffn(ya, yb, x2, p["ga"], p["gb"], p["w_out"], p["n2"], p["wu"], p["cw"], p["cb"],
                  p["wd"], seq)
    return out.reshape(batch, seq, D_MODEL)


def kernel(x_prompt, x_sample, norm1, w_in, q_norm_a, k_norm_a, q_norm_b, k_norm_b, sink_b, out_norm_a,
           out_norm_b, w_out, norm2, w_up, conv_w, conv_b, w_down):
    y_prompt, y_sample = x_prompt, x_sample
    for l in range(norm1.shape[0]):
        p = _prepare(norm1[l], w_in[l], q_norm_a[l], k_norm_a[l], q_norm_b[l], k_norm_b[l], sink_b[l],
                     out_norm_a[l], out_norm_b[l], w_out[l], norm2[l], w_up[l], conv_w[l], conv_b[l],
                     w_down[l])
        y_prompt = _layer(y_prompt, p)
        y_sample = _layer(y_sample, p)
    return (y_prompt, y_sample)
```

```python
import functools

import numpy as np
import jax
import jax.numpy as jnp
from jax import lax
from jax.experimental import pallas as pl
from jax.experimental.pallas import tpu as pltpu

F32 = jnp.float32
BF16 = jnp.bfloat16

D_MODEL = 1024
HEAD_DIM = 64
PAIR = 2 * HEAD_DIM
N_PAIRS = 4
WIDTH_A = 512
WIDTH_B = 512
KV_WIDTH_B = 128
IN_WIDTH = 2304
DILATIONS = (1, 4, 16)
MERGE_STRIDE = 4
RADIUS_A = 64
RADIUS_B = 128
D_FF = 2816
EPS = 1e-6
MASKED = 1e30
LOG2E = 1.4426950408889634

Q_BLOCK = 128
ROW_GROUP = 16
SCORE_AHEAD = 2
TOKEN_TILE_IN = 1024
TOKEN_TILE_ATTN = 2048
TOKEN_TILE_FFN = 512
FFN_HALO = 16
FF_CHUNK = 256
N_FF_CHUNKS = D_FF // FF_CHUNK
VMEM_LIMIT = 56 * 1024 * 1024


def _alibi_slopes():
    n = 16
    h = np.arange(1, n + 1, dtype=np.float32)
    s = np.exp2(-8.0 * h / n).astype(np.float32)
    return s[8:], s[:8]


def _inproj_kernel(x_ref, n1_ref, w_ref, rep_ref, gqa_ref, gka_ref, gqb_ref, gkb_ref,
                   qa1_ref, qa4_ref, qa16_ref, ka1_ref, ka4_ref, ka16_ref, va1_ref, va4_ref, va16_ref,
                   qb_ref, kb_ref, vb_ref, stage_ref, stage4_ref):
    t = TOKEN_TILE_IN
    x = x_ref[...]
    ms = jnp.mean(x * x, axis=-1, keepdims=True)
    h = ((x * lax.rsqrt(ms + EPS)) * n1_ref[...]).astype(BF16)
    lane = lax.broadcasted_iota(jnp.int32, (1, PAIR), 1)
    head0 = lane < HEAD_DIM

    def proj(c0, n):
        return jnp.dot(h, w_ref[:, c0:c0 + n], preferred_element_type=F32)

    def head_norm(p, g):
        slabs = []
        for c in range(0, p.shape[1], PAIR):
            sq = p[:, c:c + PAIR] * p[:, c:c + PAIR]
            lo = jnp.sum(jnp.where(head0, sq, 0.0), axis=-1, keepdims=True)
            hi = jnp.sum(jnp.where(head0, 0.0, sq), axis=-1, keepdims=True)
            slabs.append(jnp.where(head0, lo, hi) * (1.0 / HEAD_DIM))
        msq = slabs[0] if len(slabs) == 1 else jnp.concatenate(slabs, axis=1)
        return (p * lax.rsqrt(msq + EPS)) * g

    def write_pairs(ref, y, first_pair):
        for j in range(y.shape[1] // PAIR):
            ref[first_pair + j] = y[:, j * PAIR:(j + 1) * PAIR].astype(BF16)

    def write_dilated(refs, y, first_pair, first_slot):
        ref1, ref4, ref16 = refs
        _, d4, d16 = DILATIONS
        sub = d16 // d4
        for j in range(y.shape[1] // PAIR):
            pair = first_pair + j
            yp = y[:, j * PAIR:(j + 1) * PAIR]
            ref1[pair] = yp.astype(BF16)
            stage = stage_ref.at[first_slot + j]
            stage4 = stage4_ref.at[first_slot + j]
            stage[...] = yp
            for r in range(d4):
                rows = stage[pl.ds(r, t // d4, stride=d4), :]
                stage4[r] = rows
                ref4[pair, :, r * PAIR:(r + 1) * PAIR] = rows.astype(BF16)
            for b in range(d16):
                rows = stage4[b % d4, pl.ds(b // d4, t // d16, stride=sub), :]
                ref16[pair, :, b * PAIR:(b + 1) * PAIR] = rows.astype(BF16)

    qa_refs = (qa1_ref, qa4_ref, qa16_ref)
    ka_refs = (ka1_ref, ka4_ref, ka16_ref)
    va_refs = (va1_ref, va4_ref, va16_ref)

    def head_norm_wide(p, g):
        halves = [head_norm(p[:, c:c + 256], g) for c in (0, 256)]
        return jnp.concatenate(halves, axis=1)

    write_dilated(qa_refs, head_norm_wide(proj(0, 512), gqa_ref[...]), 0, 0)
    write_dilated(ka_refs, head_norm_wide(proj(512, 512), gka_ref[...]), 0, 4)
    write_dilated(va_refs, proj(1024, 512), 0, 8)
    write_pairs(qb_ref, head_norm_wide(proj(1536, 512), gqb_ref[...]), 0)

    rep = rep_ref[...]
    kvb = proj(2048, 2 * KV_WIDTH_B)
    kb = head_norm(kvb[:, :KV_WIDTH_B], gkb_ref[...]).astype(BF16)
    vb = kvb[:, KV_WIDTH_B:].astype(BF16)
    write_pairs(kb_ref, jnp.dot(kb, rep, preferred_element_type=F32), 0)
    write_pairs(vb_ref, jnp.dot(vb, rep, preferred_element_type=F32), 0)


def _inproj(x2, n1, w_in, rep, gqa, gka, gqb, gkb):
    n = x2.shape[0]
    t = TOKEN_TILE_IN
    const = lambda i: (0, 0)
    pair_out = lambda p, d=1: pl.BlockSpec((p, t // d, d * PAIR), lambda i: (0, i, 0))
    pair_shape = lambda p, d=1: jax.ShapeDtypeStruct((p, n // d, d * PAIR), BF16)
    dilated = [(N_PAIRS, d) for _ in range(3) for d in DILATIONS]
    outs = dilated + [(N_PAIRS, 1), (2, 1), (2, 1)]
    res = pl.pallas_call(
        _inproj_kernel,
        grid=(n // t,),
        in_specs=[
            pl.BlockSpec((t, D_MODEL), lambda i: (i, 0)),
            pl.BlockSpec((1, D_MODEL), const),
            pl.BlockSpec((D_MODEL, IN_WIDTH), const, pipeline_mode=pl.Buffered(1)),
            pl.BlockSpec((PAIR, 256), const),
            pl.BlockSpec((1, 256), const),
            pl.BlockSpec((1, 256), const),
            pl.BlockSpec((1, 256), const),
            pl.BlockSpec((1, PAIR), const),
        ],
        out_specs=[pair_out(p, d) for p, d in outs],
        out_shape=[pair_shape(p, d) for p, d in outs],
        scratch_shapes=[pltpu.VMEM((3 * N_PAIRS, t, PAIR), F32),
                        pltpu.VMEM((3 * N_PAIRS, DILATIONS[1], t // DILATIONS[1], PAIR), F32)],
        compiler_params=pltpu.CompilerParams(
            dimension_semantics=("arbitrary",), vmem_limit_bytes=VMEM_LIMIT),
        name="inproj",
    )(x2, n1, w_in, rep, gqa, gka, gqb, gkb)
    return res[0:3], res[3:6], res[6:9], res[9], res[10], res[11]


def _build_bias(bias_ref, cfg, slopes, radius, dist_scale):
    width = Q_BLOCK + 2 * radius
    row = lax.broadcasted_iota(jnp.int32, (Q_BLOCK, width), 0)
    col = lax.broadcasted_iota(jnp.int32, (Q_BLOCK, width), 1)
    rel = col - radius - row
    band = jnp.abs(rel) <= radius
    dist = jnp.abs(rel).astype(F32) * dist_scale
    for variant in range(4):
        ok = band
        if variant & 1:
            ok = ok & (col >= radius)
        if variant & 2:
            ok = ok & (col < Q_BLOCK + radius)
        base = jnp.where(ok, dist, MASKED)
        n = len(slopes)
        for h, slope in enumerate(slopes):
            scaled = base * (slope * LOG2E)
            for g in range(Q_BLOCK // ROW_GROUP):
                r0 = (g * n + h) * ROW_GROUP
                bias_ref[cfg, variant, r0:r0 + ROW_GROUP] = scaled[g * ROW_GROUP:(g + 1) * ROW_GROUP]


def _edge_variant(qb, n_blocks, i, n_tiles):
    variant = 0
    if qb == 0:
        variant = variant + (i == 0).astype(jnp.int32)
    if qb == n_blocks - 1:
        variant = variant + 2 * (i == n_tiles - 1).astype(jnp.int32)
    return variant


def _window(refs, q0, rows, cols, radius):
    prev_ref, cur_ref, next_ref = refs
    lo, hi = q0 - radius, q0 + Q_BLOCK + radius
    parts = []
    if lo < 0:
        parts.append(prev_ref[:, cols])
    parts.append(cur_ref[max(lo, 0):min(hi, rows), cols])
    if hi > rows:
        parts.append(next_ref[:, cols])
    return parts[0] if len(parts) == 1 else jnp.concatenate(parts, axis=0)


def _scores(qblks, kwin, bias_ref, cfg, variant):
    lane = lax.broadcasted_iota(jnp.int32, (1, PAIR), 1)
    head0 = lane < HEAD_DIM
    keep = (head0.astype(BF16), (~head0).astype(BF16))
    q = jnp.concatenate([qblk[g * ROW_GROUP:(g + 1) * ROW_GROUP] * k
                         for g in range(Q_BLOCK // ROW_GROUP) for qblk in qblks for k in keep], axis=0)
    s = lax.dot_general(q, kwin, (((1,), (1,)), ((), ())), preferred_element_type=F32)
    return s - bias_ref[cfg, variant]


def _attend(s, n_pairs, vwin, sinks=None):
    lane = lax.broadcasted_iota(jnp.int32, (1, PAIR), 1)
    head0 = lane < HEAD_DIM
    m = jnp.max(s, axis=-1, keepdims=True)
    if sinks is not None:
        m = jnp.maximum(m, jnp.concatenate([jnp.full((ROW_GROUP, 1), sk, F32)
                                            for _ in range(Q_BLOCK // ROW_GROUP) for sk in sinks], axis=0))
    p = jnp.exp2(s - m).astype(BF16)
    rhs = jnp.concatenate([vwin, jnp.ones_like(vwin)], axis=1)
    o = jnp.dot(p, rhs, preferred_element_type=F32)
    m_b = jnp.broadcast_to(m, (s.shape[0], PAIR))
    out = []
    heads = 2 * n_pairs
    for k in range(n_pairs):
        def pick(a, k=k):
            parts = []
            for g in range(Q_BLOCK // ROW_GROUP):
                r0 = (g * heads + 2 * k) * ROW_GROUP
                parts.append(jnp.where(head0, a[r0:r0 + ROW_GROUP], a[r0 + ROW_GROUP:r0 + 2 * ROW_GROUP]))
            return jnp.concatenate(parts, axis=0)
        out.append((pick(o[:, :PAIR]), pick(o[:, PAIR:]), pick(m_b)))
    return out


def _mixer_a_kernel(slopes_ref, *refs, n_tiles):
    nd = len(DILATIONS)
    q_refs = refs[0:nd]
    k_refs = refs[nd:4 * nd]
    v_refs = refs[4 * nd:7 * nd]
    o_ref = refs[7 * nd]
    bias_ref, acc_ref, den_ref, max_ref = refs[7 * nd + 1:]

    hp = pl.program_id(0)
    b = pl.program_id(1)
    i = pl.program_id(2)
    t = TOKEN_TILE_ATTN
    slab = t // MERGE_STRIDE

    @pl.when((b == 0) & (i == 0))
    def _():
        for c, d in enumerate(DILATIONS):
            _build_bias(bias_ref, c, (slopes_ref[2 * hp], slopes_ref[2 * hp + 1]), RADIUS_A, float(d))

    for c, d in reversed(list(enumerate(DILATIONS))):
        rows_d = t // d
        n_blocks = rows_d // Q_BLOCK
        for qb in range(n_blocks):
            q0 = qb * Q_BLOCK
            variant = _edge_variant(qb, n_blocks, i, n_tiles)
            for r in range(d):
                cols = slice(r * PAIR, (r + 1) * PAIR)
                qblk = q_refs[c][q0:q0 + Q_BLOCK, cols]
                kwin = _window(k_refs[3 * c:3 * c + 3], q0, rows_d, cols, RADIUS_A)
                vwin = _window(v_refs[3 * c:3 * c + 3], q0, rows_d, cols, RADIUS_A)
                (acc, den, mx), = _attend(_scores([qblk], kwin, bias_ref, c, variant), 1, vwin)
                if d == 1:
                    rows = pl.ds(q0, Q_BLOCK)
                elif d == MERGE_STRIDE:
                    rows = pl.ds(r * slab + q0, Q_BLOCK)
                else:
                    sub = d // MERGE_STRIDE
                    rows = pl.ds((r % MERGE_STRIDE) * slab + r // MERGE_STRIDE + sub * q0, Q_BLOCK,
                                 stride=sub)
                acc_ref[c, rows, :] = acc
                den_ref[c, rows, :] = den
                max_ref[c, rows, :] = mx

    chunk = 256
    for i0 in range(0, slab, chunk):
        for r in range(MERGE_STRIDE):
            token_rows = pl.ds(r + MERGE_STRIDE * i0, chunk, stride=MERGE_STRIDE)
            slab_rows = pl.ds(r * slab + i0, chunk)
            rows = [token_rows if d == 1 else slab_rows for d in DILATIONS]
            ms = [max_ref[c, rows[c], :] for c in range(nd)]
            top = functools.reduce(jnp.maximum, ms)
            num = None
            den = None
            for c in range(nd):
                w = jnp.exp2(ms[c] - top)
                n_c = w * acc_ref[c, rows[c], :]
                d_c = w * den_ref[c, rows[c], :]
                num = n_c if num is None else num + n_c
                den = d_c if den is None else den + d_c
            o_ref[token_rows, :] = num / den


def _halo_specs(d, rows, total_rows, radius, lead_map):
    per = rows // radius
    last = total_rows // radius - 1
    width = d * PAIR

    def prev(*g):
        return (*lead_map(*g), jnp.maximum(g[-1] * per - 1, 0), 0)

    def cur(*g):
        return (*lead_map(*g), g[-1], 0)

    def nxt(*g):
        return (*lead_map(*g), jnp.minimum((g[-1] + 1) * per, last), 0)

    return [pl.BlockSpec((None, None, radius, width), prev),
            pl.BlockSpec((None, None, rows, width), cur),
            pl.BlockSpec((None, None, radius, width), nxt)]


def _mixer_a(qa, ka, va, slopes, batch, seq):
    t = TOKEN_TILE_ATTN
    n_tiles = seq // t
    lead = lambda hp, b, i: (hp, b)
    views = lambda a: [v.reshape(N_PAIRS, batch, seq // d, d * PAIR) for v, d in zip(a, DILATIONS)]
    in_specs = [pl.BlockSpec(memory_space=pltpu.SMEM)]
    for d in DILATIONS:
        in_specs.append(pl.BlockSpec((None, None, t // d, d * PAIR), lambda hp, b, i: (hp, b, i, 0)))
    halo = []
    for d in DILATIONS:
        halo += _halo_specs(d, t // d, seq // d, RADIUS_A, lead)
    in_specs += halo + halo
    k_views = [v for kv in views(ka) for v in (kv, kv, kv)]
    v_views = [v for vv in views(va) for v in (vv, vv, vv)]
    scratch = [pltpu.VMEM((len(DILATIONS), 4, 2 * Q_BLOCK, Q_BLOCK + 2 * RADIUS_A), F32)]
    scratch += [pltpu.VMEM((len(DILATIONS), t, PAIR), F32)] * 3
    return pl.pallas_call(
        functools.partial(_mixer_a_kernel, n_tiles=n_tiles),
        grid=(N_PAIRS, batch, n_tiles),
        in_specs=in_specs,
        out_specs=pl.BlockSpec((None, t, PAIR), lambda hp, b, i: (b, i, hp)),
        out_shape=jax.ShapeDtypeStruct((batch, seq, WIDTH_A), F32),
        scratch_shapes=scratch,
        compiler_params=pltpu.CompilerParams(
            dimension_semantics=("arbitrary", "arbitrary", "arbitrary"), vmem_limit_bytes=VMEM_LIMIT),
        name="mixer_a",
    )(slopes, *views(qa), *k_views, *v_views)


def _mixer_b_kernel(slopes_ref, sink_ref, q0_ref, q1_ref, kp_ref, kc_ref, kn_ref, vp_ref, vc_ref, vn_ref,
                    o_ref, bias_ref, s_ref, *, n_tiles):
    g = pl.program_id(0)
    b = pl.program_id(1)
    i = pl.program_id(2)
    t = TOKEN_TILE_ATTN
    n_blocks = t // Q_BLOCK
    heads = [4 * g + h for h in range(4)]

    @pl.when((b == 0) & (i == 0))
    def _():
        _build_bias(bias_ref, 0, [slopes_ref[h] for h in heads], RADIUS_B, 1.0)

    sinks = [sink_ref[h] * LOG2E for h in heads]
    lane = lax.broadcasted_iota(jnp.int32, (1, PAIR), 1)
    every = slice(None)

    def scores(qb):
        q0 = qb * Q_BLOCK
        variant = _edge_variant(qb, n_blocks, i, n_tiles)
        qblks = [q_ref[q0:q0 + Q_BLOCK, :] for q_ref in (q0_ref, q1_ref)]
        kwin = _window((kp_ref, kc_ref, kn_ref), q0, t, every, RADIUS_B)
        s_ref[qb % (SCORE_AHEAD + 1)] = _scores(qblks, kwin, bias_ref, 0, variant)

    for qb in range(SCORE_AHEAD):
        scores(qb)
    for qb in range(n_blocks):
        q0 = qb * Q_BLOCK
        if qb + SCORE_AHEAD < n_blocks:
            scores(qb + SCORE_AHEAD)
        vwin = _window((vp_ref, vc_ref, vn_ref), q0, t, every, RADIUS_B)
        stats = _attend(s_ref[qb % (SCORE_AHEAD + 1)], 2, vwin, sinks=sinks)
        for k, (acc, den, mx) in enumerate(stats):
            sink_pair = jnp.where(lane < HEAD_DIM, sinks[2 * k], sinks[2 * k + 1])
            den = den + jnp.exp2(sink_pair - mx)
            o_ref[q0:q0 + Q_BLOCK, k * PAIR:(k + 1) * PAIR] = acc / den


def _mixer_b(qb, kb, vb, slopes, sinks, batch, seq):
    t = TOKEN_TILE_ATTN
    n_tiles = seq // t
    kv_lead = lambda g, b, i: (g, b)
    view = lambda a: a.reshape(a.shape[0], batch, seq, PAIR)
    halo = _halo_specs(1, t, seq, RADIUS_B, kv_lead)
    smem = pl.BlockSpec(memory_space=pltpu.SMEM)
    q_spec = lambda k: pl.BlockSpec((None, None, t, PAIR), lambda g, b, i: (2 * g + k, b, i, 0))
    return pl.pallas_call(
        functools.partial(_mixer_b_kernel, n_tiles=n_tiles),
        grid=(2, batch, n_tiles),
        in_specs=[smem, smem, q_spec(0), q_spec(1)] + halo + halo,
        out_specs=pl.BlockSpec((None, t, 2 * PAIR), lambda g, b, i: (b, i, g)),
        out_shape=jax.ShapeDtypeStruct((batch, seq, WIDTH_B), F32),
        scratch_shapes=[pltpu.VMEM((1, 4, 4 * Q_BLOCK, Q_BLOCK + 2 * RADIUS_B), F32),
                        pltpu.VMEM((SCORE_AHEAD + 1, 4 * Q_BLOCK, Q_BLOCK + 2 * RADIUS_B), F32)],
        compiler_params=pltpu.CompilerParams(
            dimension_semantics=("arbitrary", "arbitrary", "arbitrary"), vmem_limit_bytes=VMEM_LIMIT),
        name="mixer_b",
    )(slopes, sinks, view(qb), view(qb), *([view(kb)] * 3), *([view(vb)] * 3))


def _rms(x, g):
    ms = jnp.mean(x * x, axis=-1, keepdims=True)
    return (x * lax.rsqrt(ms + EPS)) * g


def _outffn_kernel(yap_ref, yac_ref, yan_ref, ybp_ref, ybc_ref, ybn_ref, xp_ref, xc_ref, xn_ref,
                   ga_ref, gb_ref, wo_ref, n2_ref, wu_ref, cw_ref, cb_ref,
                   wd_ref, o_ref, y_ref, h2_ref, perm_ref, hid_ref, *, tiles_per_seq):
    i = pl.program_id(0)
    t = TOKEN_TILE_FFN
    halo = FFN_HALO
    first = (i % tiles_per_seq) == 0
    last = (i % tiles_per_seq) == tiles_per_seq - 1

    for r0, ya_ref, yb_ref in ((0, yap_ref, ybp_ref), (halo, yac_ref, ybc_ref), (halo + t, yan_ref, ybn_ref)):
        n = ya_ref.shape[0]
        y_ref[r0:r0 + n, 0:WIDTH_A] = _rms(ya_ref[...], ga_ref[...]).astype(BF16)
        y_ref[r0:r0 + n, WIDTH_A:] = _rms(yb_ref[...], gb_ref[...]).astype(BF16)
    mixed = jnp.dot(y_ref[...], wo_ref[...], preferred_element_type=F32)

    h_prev = jnp.where(first, 0.0, _rms(xp_ref[...] + mixed[0:halo], n2_ref[...]))
    h_next = jnp.where(last, 0.0, _rms(xn_ref[...] + mixed[halo + t:], n2_ref[...]))
    x1 = xc_ref[...] + mixed[halo:halo + t]
    o_ref[...] = x1

    per = t // 8
    h_cur = _rms(x1, n2_ref[...])
    lane_tiles = D_MODEL // 128
    for c in range(lane_tiles):
        for s in range(8):
            perm_ref[c, pl.ds(s, per, stride=8), :] = h_cur[s * per:(s + 1) * per, c * 128:(c + 1) * 128]
        h2_ref[0:t, c * 128:(c + 1) * 128] = perm_ref[c].astype(BF16)
    h2_ref[t:, :] = jnp.concatenate([h_prev[halo - 8:], h_next[:8]], axis=0).astype(BF16)

    sub = lax.broadcasted_iota(jnp.int32, (8, FF_CHUNK), 0)

    def conv(u, w, b):
        body, edge = u[0:t], u[t:]
        lo = jnp.where(sub == 0, pltpu.roll(edge[0:8], 1, axis=0), pltpu.roll(body[t - 8:], 1, axis=0))
        hi = jnp.where(sub == 7, pltpu.roll(edge[8:16], 7, axis=0), pltpu.roll(body[0:8], 7, axis=0))
        below = jnp.concatenate([lo, body[:t - 8]], axis=0)
        above = jnp.concatenate([body[8:], hi], axis=0)
        return b + below * w[0:1] + body * w[1:2] + above * w[2:3]

    for j in range(N_FF_CHUNKS):
        h2 = h2_ref[...]
        branch = []
        for c0 in (j * FF_CHUNK, D_FF + j * FF_CHUNK):
            cols = slice(c0, c0 + FF_CHUNK)
            u = jnp.dot(h2, wu_ref[:, cols], preferred_element_type=F32)
            branch.append(conv(u, cw_ref[:, cols], cb_ref[:, cols]))
        gate, val = branch
        hid_ref[:, j * FF_CHUNK:(j + 1) * FF_CHUNK] = (jax.nn.silu(gate) * val).astype(BF16)

    down = jnp.dot(hid_ref[...], wd_ref[...], preferred_element_type=F32)
    for c in range(lane_tiles):
        perm_ref[c] = down[:, c * 128:(c + 1) * 128]
        for s in range(8):
            o_ref[s * per:(s + 1) * per, c * 128:(c + 1) * 128] += perm_ref[c, pl.ds(s, per, stride=8), :]


def _outffn(ya, yb, x2, ga, gb, w_out, n2, wu, cw, cb, wd, seq):
    n = x2.shape[0]
    t = TOKEN_TILE_FFN
    per = t // FFN_HALO
    last = n // FFN_HALO - 1
    prev = lambda i: (jnp.maximum(i * per - 1, 0), 0)
    cur = lambda i: (i, 0)
    nxt = lambda i: (jnp.minimum((i + 1) * per, last), 0)

    def halo_specs(width):
        return [pl.BlockSpec((FFN_HALO, width), prev), pl.BlockSpec((t, width), cur),
                pl.BlockSpec((FFN_HALO, width), nxt)]

    def resident(shape):
        return pl.BlockSpec(shape, lambda i: (0,) * len(shape), pipeline_mode=pl.Buffered(1))

    return pl.pallas_call(
        functools.partial(_outffn_kernel, tiles_per_seq=seq // t),
        grid=(n // t,),
        in_specs=halo_specs(WIDTH_A) + halo_specs(WIDTH_B) + halo_specs(D_MODEL) + [
            resident((1, WIDTH_A)), resident((1, WIDTH_B)), resident((D_MODEL, D_MODEL)),
            resident((1, D_MODEL)),
            resident((D_MODEL, 2 * D_FF)), resident((3, 2 * D_FF)), resident((1, 2 * D_FF)),
            resident((D_FF, D_MODEL)),
        ],
        out_specs=pl.BlockSpec((t, D_MODEL), cur),
        out_shape=jax.ShapeDtypeStruct((n, D_MODEL), F32),
        scratch_shapes=[pltpu.VMEM((t + 2 * FFN_HALO, D_MODEL), BF16),
                        pltpu.VMEM((t + FFN_HALO, D_MODEL), BF16),
                        pltpu.VMEM((D_MODEL // 128, t, 128), F32),
                        pltpu.VMEM((t, D_FF), BF16)],
        compiler_params=pltpu.CompilerParams(
            dimension_semantics=("arbitrary",), vmem_limit_bytes=VMEM_LIMIT),
        name="outffn",
    )(ya, ya, ya, yb, yb, yb, x2, x2, x2, ga, gb, w_out, n2, wu, cw, cb, wd)


def _prepare(norm1, w_in, q_norm_a, k_norm_a, q_norm_b, k_norm_b, sink_b, out_norm_a, out_norm_b, w_out,
             norm2, w_up, conv_w, conv_b, w_down):
    scale = HEAD_DIM ** -0.5 * LOG2E
    lanes = np.arange(256)
    src = np.arange(PAIR)
    rep = ((src[:, None] // HEAD_DIM == lanes[None, :] // PAIR)
           & (src[:, None] % HEAD_DIM == lanes[None, :] % HEAD_DIM)).astype(np.float32)
    slopes_a, slopes_b = _alibi_slopes()

    return dict(
        n1=norm1.reshape(1, D_MODEL),
        w_in=w_in.astype(BF16),
        rep=jnp.asarray(rep, BF16),
        gqa=(jnp.tile(q_norm_a, 4) * scale).reshape(1, 256),
        gka=jnp.tile(k_norm_a, 4).reshape(1, 256),
        gqb=(jnp.tile(q_norm_b, 4) * scale).reshape(1, 256),
        gkb=jnp.tile(k_norm_b, 2).reshape(1, PAIR),
        slopes_a=jnp.asarray(slopes_a),
        slopes_b=jnp.asarray(slopes_b),
        sinks=sink_b.astype(F32),
        ga=out_norm_a.reshape(1, WIDTH_A),
        gb=out_norm_b.reshape(1, WIDTH_B),
        w_out=w_out.astype(BF16),
        n2=norm2.reshape(1, D_MODEL),
        wu=w_up.astype(BF16),
        cw=conv_w,
        cb=conv_b.reshape(1, 2 * D_FF),
        wd=w_down.astype(BF16),
    )


def _layer(x, p):
    batch, seq, _ = x.shape
    assert seq % TOKEN_TILE_ATTN == 0 and seq % TOKEN_TILE_FFN == 0
    x2 = x.reshape(batch * seq, D_MODEL)
    qa, ka, va, qb, kb, vb = _inproj(x2, p["n1"], p["w_in"], p["rep"],
                                     p["gqa"], p["gka"], p["gqb"], p["gkb"])
    ya = _mixer_a(qa, ka, va, p["slopes_a"], batch, seq).reshape(batch * seq, WIDTH_A)
    yb = _mixer_b(qb, kb, vb, p["slopes_b"], p["sinks"], batch, seq).reshape(batch * seq, WIDTH_B)
    out = _outffn(ya, yb, x2, p["ga"], p["gb"], p["w_out"], p["n2"], p["wu"], p["cw"], p["cb"],
                  p["wd"], seq)
    return out.reshape(batch, seq, D_MODEL)


def kernel(x_prompt, x_sample, norm1, w_in, q_norm_a, k_norm_a, q_norm_b, k_norm_b, sink_b, out_norm_a,
           out_norm_b, w_out, norm2, w_up, conv_w, conv_b, w_down):
    y_prompt, y_sample = x_prompt, x_sample
    for l in range(norm1.shape[0]):
        p = _prepare(norm1[l], w_in[l], q_norm_a[l], k_norm_a[l], q_norm_b[l], k_norm_b[l], sink_b[l],
                     out_norm_a[l], out_norm_b[l], w_out[l], norm2[l], w_up[l], conv_w[l], conv_b[l],
                     w_down[l])
        y_prompt = _layer(y_prompt, p)
        y_sample = _layer(y_sample, p)
    return (y_prompt, y_sample)
```

```python
import functools

import numpy as np
import jax
import jax.numpy as jnp
from jax import lax
from jax.experimental import pallas as pl
from jax.experimental.pallas import tpu as pltpu

F32 = jnp.float32
BF16 = jnp.bfloat16

D_MODEL = 1024
HEAD_DIM = 64
PAIR = 2 * HEAD_DIM
N_PAIRS = 4
WIDTH_A = 512
WIDTH_B = 512
KV_WIDTH_B = 128
IN_WIDTH = 2304
DILATIONS = (1, 4, 16)
MERGE_STRIDE = 4
RADIUS_A = 64
RADIUS_B = 128
D_FF = 2816
EPS = 1e-6
MASKED = 1e30
LOG2E = 1.4426950408889634

Q_BLOCK = 128
ROW_GROUP = 16
SCORE_AHEAD = 2
TOKEN_TILE_IN = 1024
TOKEN_TILE_ATTN = 2048
TOKEN_TILE_FFN = 512
FFN_HALO = 16
FF_CHUNK = 256
N_FF_CHUNKS = D_FF // FF_CHUNK
VMEM_LIMIT = 56 * 1024 * 1024


def _alibi_slopes():
    n = 16
    h = np.arange(1, n + 1, dtype=np.float32)
    s = np.exp2(-8.0 * h / n).astype(np.float32)
    return s[8:], s[:8]


def _inproj_kernel(x_ref, n1_ref, w_ref, rep_ref, gqa_ref, gka_ref, gqb_ref, gkb_ref,
                   qa1_ref, qa4_ref, qa16_ref, ka1_ref, ka4_ref, ka16_ref, va1_ref, va4_ref, va16_ref,
                   qb_ref, kb_ref, vb_ref, stage_ref, stage4_ref):
    t = TOKEN_TILE_IN
    x = x_ref[...]
    ms = jnp.mean(x * x, axis=-1, keepdims=True)
    h = ((x * lax.rsqrt(ms + EPS)) * n1_ref[...]).astype(BF16)
    lane = lax.broadcasted_iota(jnp.int32, (1, PAIR), 1)
    head0 = lane < HEAD_DIM

    def proj(c0, n):
        return jnp.dot(h, w_ref[:, c0:c0 + n], preferred_element_type=F32)

    def head_norm(p, g):
        slabs = []
        for c in range(0, p.shape[1], PAIR):
            sq = p[:, c:c + PAIR] * p[:, c:c + PAIR]
            lo = jnp.sum(jnp.where(head0, sq, 0.0), axis=-1, keepdims=True)
            hi = jnp.sum(jnp.where(head0, 0.0, sq), axis=-1, keepdims=True)
            slabs.append(jnp.where(head0, lo, hi) * (1.0 / HEAD_DIM))
        msq = slabs[0] if len(slabs) == 1 else jnp.concatenate(slabs, axis=1)
        return (p * lax.rsqrt(msq + EPS)) * g

    def write_pairs(ref, y, first_pair):
        for j in range(y.shape[1] // PAIR):
            ref[first_pair + j] = y[:, j * PAIR:(j + 1) * PAIR].astype(BF16)

    def write_dilated(refs, y, first_pair, first_slot):
        ref1, ref4, ref16 = refs
        _, d4, d16 = DILATIONS
        sub = d16 // d4
        for j in range(y.shape[1] // PAIR):
            pair = first_pair + j
            yp = y[:, j * PAIR:(j + 1) * PAIR]
            ref1[pair] = yp.astype(BF16)
            stage = stage_ref.at[first_slot + j]
            stage4 = stage4_ref.at[first_slot + j]
            stage[...] = yp
            for r in range(d4):
                rows = stage[pl.ds(r, t // d4, stride=d4), :]
                stage4[r] = rows
                ref4[pair, :, r * PAIR:(r + 1) * PAIR] = rows.astype(BF16)
            for b in range(d16):
                rows = stage4[b % d4, pl.ds(b // d4, t // d16, stride=sub), :]
                ref16[pair, :, b * PAIR:(b + 1) * PAIR] = rows.astype(BF16)

    qa_refs = (qa1_ref, qa4_ref, qa16_ref)
    ka_refs = (ka1_ref, ka4_ref, ka16_ref)
    va_refs = (va1_ref, va4_ref, va16_ref)

    def head_norm_wide(p, g):
        halves = [head_norm(p[:, c:c + 256], g) for c in (0, 256)]
        return jnp.concatenate(halves, axis=1)

    write_dilated(qa_refs, head_norm_wide(proj(0, 512), gqa_ref[...]), 0, 0)
    write_dilated(ka_refs, head_norm_wide(proj(512, 512), gka_ref[...]), 0, 4)
    write_dilated(va_refs, proj(1024, 512), 0, 8)
    write_pairs(qb_ref, head_norm_wide(proj(1536, 512), gqb_ref[...]), 0)

    rep = rep_ref[...]
    kvb = proj(2048, 2 * KV_WIDTH_B)
    kb = head_norm(kvb[:, :KV_WIDTH_B], gkb_ref[...]).astype(BF16)
    vb = kvb[:, KV_WIDTH_B:].astype(BF16)
    write_pairs(kb_ref, jnp.dot(kb, rep, preferred_element_type=F32), 0)
    write_pairs(vb_ref, jnp.dot(vb, rep, preferred_element_type=F32), 0)


def _inproj(x2, n1, w_in, rep, gqa, gka, gqb, gkb):
    n = x2.shape[0]
    t = TOKEN_TILE_IN
    const = lambda i: (0, 0)
    pair_out = lambda p, d=1: pl.BlockSpec((p, t // d, d * PAIR), lambda i: (0, i, 0))
    pair_shape = lambda p, d=1: jax.ShapeDtypeStruct((p, n // d, d * PAIR), BF16)
    dilated = [(N_PAIRS, d) for _ in range(3) for d in DILATIONS]
    outs = dilated + [(N_PAIRS, 1), (2, 1), (2, 1)]
    res = pl.pallas_call(
        _inproj_kernel,
        grid=(n // t,),
        in_specs=[
            pl.BlockSpec((t, D_MODEL), lambda i: (i, 0)),
            pl.BlockSpec((1, D_MODEL), const),
            pl.BlockSpec((D_MODEL, IN_WIDTH), const, pipeline_mode=pl.Buffered(1)),
            pl.BlockSpec((PAIR, 256), const),
            pl.BlockSpec((1, 256), const),
            pl.BlockSpec((1, 256), const),
            pl.BlockSpec((1, 256), const),
            pl.BlockSpec((1, PAIR), const),
        ],
        out_specs=[pair_out(p, d) for p, d in outs],
        out_shape=[pair_shape(p, d) for p, d in outs],
        scratch_shapes=[pltpu.VMEM((3 * N_PAIRS, t, PAIR), F32),
                        pltpu.VMEM((3 * N_PAIRS, DILATIONS[1], t // DILATIONS[1], PAIR), F32)],
        compiler_params=pltpu.CompilerParams(
            dimension_semantics=("arbitrary",), vmem_limit_bytes=VMEM_LIMIT),
        name="inproj",
    )(x2, n1, w_in, rep, gqa, gka, gqb, gkb)
    return res[0:3], res[3:6], res[6:9], res[9], res[10], res[11]


def _build_bias(bias_ref, cfg, slopes, radius, dist_scale):
    width = Q_BLOCK + 2 * radius
    row = lax.broadcasted_iota(jnp.int32, (Q_BLOCK, width), 0)
    col = lax.broadcasted_iota(jnp.int32, (Q_BLOCK, width), 1)
    rel = col - radius - row
    band = jnp.abs(rel) <= radius
    dist = jnp.abs(rel).astype(F32) * dist_scale
    for variant in range(4):
        ok = band
        if variant & 1:
            ok = ok & (col >= radius)
        if variant & 2:
            ok = ok & (col < Q_BLOCK + radius)
        base = jnp.where(ok, dist, MASKED)
        n = len(slopes)
        for h, slope in enumerate(slopes):
            scaled = base * (slope * LOG2E)
            for g in range(Q_BLOCK // ROW_GROUP):
                r0 = (g * n + h) * ROW_GROUP
                bias_ref[cfg, variant, r0:r0 + ROW_GROUP] = scaled[g * ROW_GROUP:(g + 1) * ROW_GROUP]


def _edge_variant(qb, n_blocks, i, n_tiles):
    variant = 0
    if qb == 0:
        variant = variant + (i == 0).astype(jnp.int32)
    if qb == n_blocks - 1:
        variant = variant + 2 * (i == n_tiles - 1).astype(jnp.int32)
    return variant


def _window(refs, q0, rows, cols, radius):
    prev_ref, cur_ref, next_ref = refs
    lo, hi = q0 - radius, q0 + Q_BLOCK + radius
    parts = []
    if lo < 0:
        parts.append(prev_ref[:, cols])
    parts.append(cur_ref[max(lo, 0):min(hi, rows), cols])
    if hi > rows:
        parts.append(next_ref[:, cols])
    return parts[0] if len(parts) == 1 else jnp.concatenate(parts, axis=0)


def _scores(qblks, kwin, bias_ref, cfg, variant):
    lane = lax.broadcasted_iota(jnp.int32, (1, PAIR), 1)
    head0 = lane < HEAD_DIM
    keep = (head0.astype(BF16), (~head0).astype(BF16))
    q = jnp.concatenate([qblk[g * ROW_GROUP:(g + 1) * ROW_GROUP] * k
                         for g in range(Q_BLOCK // ROW_GROUP) for qblk in qblks for k in keep], axis=0)
    s = lax.dot_general(q, kwin, (((1,), (1,)), ((), ())), preferred_element_type=F32)
    return s - bias_ref[cfg, variant]


def _attend(s, n_pairs, vwin, sinks=None):
    lane = lax.broadcasted_iota(jnp.int32, (1, PAIR), 1)
    head0 = lane < HEAD_DIM
    m = jnp.max(s, axis=-1, keepdims=True)
    if sinks is not None:
        m = jnp.maximum(m, jnp.concatenate([jnp.full((ROW_GROUP, 1), sk, F32)
                                            for _ in range(Q_BLOCK // ROW_GROUP) for sk in sinks], axis=0))
    p = jnp.exp2(s - m).astype(BF16)
    rhs = jnp.concatenate([vwin, jnp.ones_like(vwin)], axis=1)
    o = jnp.dot(p, rhs, preferred_element_type=F32)
    m_b = jnp.broadcast_to(m, (s.shape[0], PAIR))
    out = []
    heads = 2 * n_pairs
    for k in range(n_pairs):
        def pick(a, k=k):
            parts = []
            for g in range(Q_BLOCK // ROW_GROUP):
                r0 = (g * heads + 2 * k) * ROW_GROUP
                parts.append(jnp.where(head0, a[r0:r0 + ROW_GROUP], a[r0 + ROW_GROUP:r0 + 2 * ROW_GROUP]))
            return jnp.concatenate(parts, axis=0)
        out.append((pick(o[:, :PAIR]), pick(o[:, PAIR:]), pick(m_b)))
    return out


def _mixer_a_kernel(slopes_ref, *refs, n_tiles):
    nd = len(DILATIONS)
    q_refs = refs[0:nd]
    k_refs = refs[nd:4 * nd]
    v_refs = refs[4 * nd:7 * nd]
    o_ref = refs[7 * nd]
    bias_ref, acc_ref, den_ref, max_ref, out_ref = refs[7 * nd + 1:]

    hp = pl.program_id(0)
    b = pl.program_id(1)
    i = pl.program_id(2)
    t = TOKEN_TILE_ATTN
    slab = t // MERGE_STRIDE

    @pl.when((b == 0) & (i == 0))
    def _():
        for c, d in enumerate(DILATIONS):
            _build_bias(bias_ref, c, (slopes_ref[2 * hp], slopes_ref[2 * hp + 1]), RADIUS_A, float(d))

    for c, d in reversed(list(enumerate(DILATIONS))):
        rows_d = t // d
        n_blocks = rows_d // Q_BLOCK
        for qb in range(n_blocks):
            q0 = qb * Q_BLOCK
            variant = _edge_variant(qb, n_blocks, i, n_tiles)
            for r in range(d):
                cols = slice(r * PAIR, (r + 1) * PAIR)
                qblk = q_refs[c][q0:q0 + Q_BLOCK, cols]
                kwin = _window(k_refs[3 * c:3 * c + 3], q0, rows_d, cols, RADIUS_A)
                vwin = _window(v_refs[3 * c:3 * c + 3], q0, rows_d, cols, RADIUS_A)
                (acc, den, mx), = _attend(_scores([qblk], kwin, bias_ref, c, variant), 1, vwin)
                if d == 1:
                    rows = pl.ds(q0, Q_BLOCK)
                elif d == MERGE_STRIDE:
                    rows = pl.ds(r * slab + q0, Q_BLOCK)
                else:
                    sub = d // MERGE_STRIDE
                    rows = pl.ds((r % MERGE_STRIDE) * slab + r // MERGE_STRIDE + sub * q0, Q_BLOCK,
                                 stride=sub)
                acc_ref[c, rows, :] = acc
                den_ref[c, rows, :] = den
                max_ref[c, rows, :] = mx

    chunk = 256
    for i0 in range(0, slab, chunk):
        for r in range(MERGE_STRIDE):
            token_rows = pl.ds(r + MERGE_STRIDE * i0, chunk, stride=MERGE_STRIDE)
            slab_rows = pl.ds(r * slab + i0, chunk)
            rows = [token_rows if d == 1 else slab_rows for d in DILATIONS]
            ms = [max_ref[c, rows[c], :] for c in range(nd)]
            top = functools.reduce(jnp.maximum, ms)
            num = None
            den = None
            for c in range(nd):
                w = jnp.exp2(ms[c] - top)
                n_c = w * acc_ref[c, rows[c], :]
                d_c = w * den_ref[c, rows[c], :]
                num = n_c if num is None else num + n_c
                den = d_c if den is None else den + d_c
            out_ref[token_rows, :] = num / den
    o_ref[...] = out_ref[...].astype(o_ref.dtype)


def _halo_specs(d, rows, total_rows, radius, lead_map):
    per = rows // radius
    last = total_rows // radius - 1
    width = d * PAIR

    def prev(*g):
        return (*lead_map(*g), jnp.maximum(g[-1] * per - 1, 0), 0)

    def cur(*g):
        return (*lead_map(*g), g[-1], 0)

    def nxt(*g):
        return (*lead_map(*g), jnp.minimum((g[-1] + 1) * per, last), 0)

    return [pl.BlockSpec((None, None, radius, width), prev),
            pl.BlockSpec((None, None, rows, width), cur),
            pl.BlockSpec((None, None, radius, width), nxt)]


def _mixer_a(qa, ka, va, slopes, batch, seq):
    t = TOKEN_TILE_ATTN
    n_tiles = seq // t
    lead = lambda hp, b, i: (hp, b)
    views = lambda a: [v.reshape(N_PAIRS, batch, seq // d, d * PAIR) for v, d in zip(a, DILATIONS)]
    in_specs = [pl.BlockSpec(memory_space=pltpu.SMEM)]
    for d in DILATIONS:
        in_specs.append(pl.BlockSpec((None, None, t // d, d * PAIR), lambda hp, b, i: (hp, b, i, 0)))
    halo = []
    for d in DILATIONS:
        halo += _halo_specs(d, t // d, seq // d, RADIUS_A, lead)
    in_specs += halo + halo
    k_views = [v for kv in views(ka) for v in (kv, kv, kv)]
    v_views = [v for vv in views(va) for v in (vv, vv, vv)]
    scratch = [pltpu.VMEM((len(DILATIONS), 4, 2 * Q_BLOCK, Q_BLOCK + 2 * RADIUS_A), F32)]
    scratch += [pltpu.VMEM((len(DILATIONS), t, PAIR), F32)] * 3
    scratch += [pltpu.VMEM((t, PAIR), F32)]
    return pl.pallas_call(
        functools.partial(_mixer_a_kernel, n_tiles=n_tiles),
        grid=(N_PAIRS, batch, n_tiles),
        in_specs=in_specs,
        out_specs=pl.BlockSpec((None, t, PAIR), lambda hp, b, i: (b, i, hp)),
        out_shape=jax.ShapeDtypeStruct((batch, seq, WIDTH_A), BF16),
        scratch_shapes=scratch,
        compiler_params=pltpu.CompilerParams(
            dimension_semantics=("arbitrary", "arbitrary", "arbitrary"), vmem_limit_bytes=VMEM_LIMIT),
        name="mixer_a",
    )(slopes, *views(qa), *k_views, *v_views)


def _mixer_b_kernel(slopes_ref, sink_ref, q0_ref, q1_ref, kp_ref, kc_ref, kn_ref, vp_ref, vc_ref, vn_ref,
                    o_ref, bias_ref, s_ref, *, n_tiles):
    g = pl.program_id(0)
    b = pl.program_id(1)
    i = pl.program_id(2)
    t = TOKEN_TILE_ATTN
    n_blocks = t // Q_BLOCK
    heads = [4 * g + h for h in range(4)]

    @pl.when((b == 0) & (i == 0))
    def _():
        _build_bias(bias_ref, 0, [slopes_ref[h] for h in heads], RADIUS_B, 1.0)

    sinks = [sink_ref[h] * LOG2E for h in heads]
    lane = lax.broadcasted_iota(jnp.int32, (1, PAIR), 1)
    every = slice(None)

    def scores(qb):
        q0 = qb * Q_BLOCK
        variant = _edge_variant(qb, n_blocks, i, n_tiles)
        qblks = [q_ref[q0:q0 + Q_BLOCK, :] for q_ref in (q0_ref, q1_ref)]
        kwin = _window((kp_ref, kc_ref, kn_ref), q0, t, every, RADIUS_B)
        s_ref[qb % (SCORE_AHEAD + 1)] = _scores(qblks, kwin, bias_ref, 0, variant)

    for qb in range(SCORE_AHEAD):
        scores(qb)
    for qb in range(n_blocks):
        q0 = qb * Q_BLOCK
        if qb + SCORE_AHEAD < n_blocks:
            scores(qb + SCORE_AHEAD)
        vwin = _window((vp_ref, vc_ref, vn_ref), q0, t, every, RADIUS_B)
        stats = _attend(s_ref[qb % (SCORE_AHEAD + 1)], 2, vwin, sinks=sinks)
        for k, (acc, den, mx) in enumerate(stats):
            sink_pair = jnp.where(lane < HEAD_DIM, sinks[2 * k], sinks[2 * k + 1])
            den = den + jnp.exp2(sink_pair - mx)
            o_ref[q0:q0 + Q_BLOCK, k * PAIR:(k + 1) * PAIR] = (acc / den).astype(o_ref.dtype)


def _mixer_b(qb, kb, vb, slopes, sinks, batch, seq):
    t = TOKEN_TILE_ATTN
    n_tiles = seq // t
    kv_lead = lambda g, b, i: (g, b)
    view = lambda a: a.reshape(a.shape[0], batch, seq, PAIR)
    halo = _halo_specs(1, t, seq, RADIUS_B, kv_lead)
    smem = pl.BlockSpec(memory_space=pltpu.SMEM)
    q_spec = lambda k: pl.BlockSpec((None, None, t, PAIR), lambda g, b, i: (2 * g + k, b, i, 0))
    return pl.pallas_call(
        functools.partial(_mixer_b_kernel, n_tiles=n_tiles),
        grid=(2, batch, n_tiles),
        in_specs=[smem, smem, q_spec(0), q_spec(1)] + halo + halo,
        out_specs=pl.BlockSpec((None, t, 2 * PAIR), lambda g, b, i: (b, i, g)),
        out_shape=jax.ShapeDtypeStruct((batch, seq, WIDTH_B), BF16),
        scratch_shapes=[pltpu.VMEM((1, 4, 4 * Q_BLOCK, Q_BLOCK + 2 * RADIUS_B), F32),
                        pltpu.VMEM((SCORE_AHEAD + 1, 4 * Q_BLOCK, Q_BLOCK + 2 * RADIUS_B), F32)],
        compiler_params=pltpu.CompilerParams(
            dimension_semantics=("arbitrary", "arbitrary", "arbitrary"), vmem_limit_bytes=VMEM_LIMIT),
        name="mixer_b",
    )(slopes, sinks, view(qb), view(qb), *([view(kb)] * 3), *([view(vb)] * 3))


def _rms(x, g):
    ms = jnp.mean(x * x, axis=-1, keepdims=True)
    return (x * lax.rsqrt(ms + EPS)) * g


def _outffn_kernel(yap_ref, yac_ref, yan_ref, ybp_ref, ybc_ref, ybn_ref, xp_ref, xc_ref, xn_ref,
                   ga_ref, gb_ref, wo_ref, n2_ref, wu_ref, cw_ref, cb_ref,
                   wd_ref, o_ref, y_ref, h2_ref, perm_ref, hid_ref, *, tiles_per_seq):
    i = pl.program_id(0)
    t = TOKEN_TILE_FFN
    halo = FFN_HALO
    first = (i % tiles_per_seq) == 0
    last = (i % tiles_per_seq) == tiles_per_seq - 1

    for r0, ya_ref, yb_ref in ((0, yap_ref, ybp_ref), (halo, yac_ref, ybc_ref), (halo + t, yan_ref, ybn_ref)):
        n = ya_ref.shape[0]
        y_ref[r0:r0 + n, 0:WIDTH_A] = _rms(ya_ref[...].astype(F32), ga_ref[...]).astype(BF16)
        y_ref[r0:r0 + n, WIDTH_A:] = _rms(yb_ref[...].astype(F32), gb_ref[...]).astype(BF16)
    mixed = jnp.dot(y_ref[...], wo_ref[...], preferred_element_type=F32)

    h_prev = jnp.where(first, 0.0, _rms(xp_ref[...] + mixed[0:halo], n2_ref[...]))
    h_next = jnp.where(last, 0.0, _rms(xn_ref[...] + mixed[halo + t:], n2_ref[...]))
    x1 = xc_ref[...] + mixed[halo:halo + t]
    o_ref[...] = x1

    per = t // 8
    h_cur = _rms(x1, n2_ref[...])
    lane_tiles = D_MODEL // 128
    for c in range(lane_tiles):
        for s in range(8):
            perm_ref[c, pl.ds(s, per, stride=8), :] = h_cur[s * per:(s + 1) * per, c * 128:(c + 1) * 128]
        h2_ref[0:t, c * 128:(c + 1) * 128] = perm_ref[c].astype(BF16)
    h2_ref[t:, :] = jnp.concatenate([h_prev[halo - 8:], h_next[:8]], axis=0).astype(BF16)

    sub = lax.broadcasted_iota(jnp.int32, (8, FF_CHUNK), 0)

    def conv(u, w, b):
        body, edge = u[0:t], u[t:]
        lo = jnp.where(sub == 0, pltpu.roll(edge[0:8], 1, axis=0), pltpu.roll(body[t - 8:], 1, axis=0))
        hi = jnp.where(sub == 7, pltpu.roll(edge[8:16], 7, axis=0), pltpu.roll(body[0:8], 7, axis=0))
        below = jnp.concatenate([lo, body[:t - 8]], axis=0)
        above = jnp.concatenate([body[8:], hi], axis=0)
        return b + below * w[0:1] + body * w[1:2] + above * w[2:3]

    for j in range(N_FF_CHUNKS):
        h2 = h2_ref[...]
        branch = []
        for c0 in (j * FF_CHUNK, D_FF + j * FF_CHUNK):
            cols = slice(c0, c0 + FF_CHUNK)
            u = jnp.dot(h2, wu_ref[:, cols], preferred_element_type=F32)
            branch.append(conv(u, cw_ref[:, cols], cb_ref[:, cols]))
        gate, val = branch
        hid_ref[:, j * FF_CHUNK:(j + 1) * FF_CHUNK] = (jax.nn.silu(gate) * val).astype(BF16)

    down = jnp.dot(hid_ref[...], wd_ref[...], preferred_element_type=F32)
    for c in range(lane_tiles):
        perm_ref[c] = down[:, c * 128:(c + 1) * 128]
        for s in range(8):
            o_ref[s * per:(s + 1) * per, c * 128:(c + 1) * 128] += perm_ref[c, pl.ds(s, per, stride=8), :]


def _outffn(ya, yb, x2, ga, gb, w_out, n2, wu, cw, cb, wd, seq):
    n = x2.shape[0]
    t = TOKEN_TILE_FFN
    per = t // FFN_HALO
    last = n // FFN_HALO - 1
    prev = lambda i: (jnp.maximum(i * per - 1, 0), 0)
    cur = lambda i: (i, 0)
    nxt = lambda i: (jnp.minimum((i + 1) * per, last), 0)

    def halo_specs(width):
        return [pl.BlockSpec((FFN_HALO, width), prev), pl.BlockSpec((t, width), cur),
                pl.BlockSpec((FFN_HALO, width), nxt)]

    def resident(shape):
        return pl.BlockSpec(shape, lambda i: (0,) * len(shape), pipeline_mode=pl.Buffered(1))

    return pl.pallas_call(
        functools.partial(_outffn_kernel, tiles_per_seq=seq // t),
        grid=(n // t,),
        in_specs=halo_specs(WIDTH_A) + halo_specs(WIDTH_B) + halo_specs(D_MODEL) + [
            resident((1, WIDTH_A)), resident((1, WIDTH_B)), resident((D_MODEL, D_MODEL)),
            resident((1, D_MODEL)),
            resident((D_MODEL, 2 * D_FF)), resident((3, 2 * D_FF)), resident((1, 2 * D_FF)),
            resident((D_FF, D_MODEL)),
        ],
        out_specs=pl.BlockSpec((t, D_MODEL), cur),
        out_shape=jax.ShapeDtypeStruct((n, D_MODEL), F32),
        scratch_shapes=[pltpu.VMEM((t + 2 * FFN_HALO, D_MODEL), BF16),
                        pltpu.VMEM((t + FFN_HALO, D_MODEL), BF16),
                        pltpu.VMEM((D_MODEL // 128, t, 128), F32),
                        pltpu.VMEM((t, D_FF), BF16)],
        compiler_params=pltpu.CompilerParams(
            dimension_semantics=("arbitrary",), vmem_limit_bytes=VMEM_LIMIT),
        name="outffn",
    )(ya, ya, ya, yb, yb, yb, x2, x2, x2, ga, gb, w_out, n2, wu, cw, cb, wd)


def _prepare(norm1, w_in, q_norm_a, k_norm_a, q_norm_b, k_norm_b, sink_b, out_norm_a, out_norm_b, w_out,
             norm2, w_up, conv_w, conv_b, w_down):
    scale = HEAD_DIM ** -0.5 * LOG2E
    lanes = np.arange(256)
    src = np.arange(PAIR)
    rep = ((src[:, None] // HEAD_DIM == lanes[None, :] // PAIR)
           & (src[:, None] % HEAD_DIM == lanes[None, :] % HEAD_DIM)).astype(np.float32)
    slopes_a, slopes_b = _alibi_slopes()

    return dict(
        n1=norm1.reshape(1, D_MODEL),
        w_in=w_in.astype(BF16),
        rep=jnp.asarray(rep, BF16),
        gqa=(jnp.tile(q_norm_a, 4) * scale).reshape(1, 256),
        gka=jnp.tile(k_norm_a, 4).reshape(1, 256),
        gqb=(jnp.tile(q_norm_b, 4) * scale).reshape(1, 256),
        gkb=jnp.tile(k_norm_b, 2).reshape(1, PAIR),
        slopes_a=jnp.asarray(slopes_a),
        slopes_b=jnp.asarray(slopes_b),
        sinks=sink_b.astype(F32),
        ga=out_norm_a.reshape(1, WIDTH_A),
        gb=out_norm_b.reshape(1, WIDTH_B),
        w_out=w_out.astype(BF16),
        n2=norm2.reshape(1, D_MODEL),
        wu=w_up.astype(BF16),
        cw=conv_w,
        cb=conv_b.reshape(1, 2 * D_FF),
        wd=w_down.astype(BF16),
    )


def _layer(x, p):
    batch, seq, _ = x.shape
    assert seq % TOKEN_TILE_ATTN == 0 and seq % TOKEN_TILE_FFN == 0
    x2 = x.reshape(batch * seq, D_MODEL)
    qa, ka, va, qb, kb, vb = _inproj(x2, p["n1"], p["w_in"], p["rep"],
                                     p["gqa"], p["gka"], p["gqb"], p["gkb"])
    ya = _mixer_a(qa, ka, va, p["slopes_a"], batch, seq).reshape(batch * seq, WIDTH_A)
    yb = _mixer_b(qb, kb, vb, p["slopes_b"], p["sinks"], batch, seq).reshape(batch * seq, WIDTH_B)
    out = _outffn(ya, yb, x2, p["ga"], p["gb"], p["w_out"], p["n2"], p["wu"], p["cw"], p["cb"],
                  p["wd"], seq)
    return out.reshape(batch, seq, D_MODEL)


def kernel(x_prompt, x_sample, norm1, w_in, q_norm_a, k_norm_a, q_norm_b, k_norm_b, sink_b, out_norm_a,
           out_norm_b, w_out, norm2, w_up, conv_w, conv_b, w_down):
    y_prompt, y_sample = x_prompt, x_sample
    for l in range(norm1.shape[0]):
        p = _prepare(norm1[l], w_in[l], q_norm_a[l], k_norm_a[l], q_norm_b[l], k_norm_b[l], sink_b[l],
                     out_norm_a[l], out_norm_b[l], w_out[l], norm2[l], w_up[l], conv_w[l], conv_b[l],
                     w_down[l])
        y_prompt = _layer(y_prompt, p)
        y_sample = _layer(y_sample, p)
    return (y_prompt, y_sample)
```
